```python
import jax, jax.numpy as jnp
from jax import lax
import numpy as np

D_MODEL = 4096
BATCH = 4
SEQ = 2048
DEPTH = 1
DEC_BATCH = 128
DEC_SEQ = 4
PAST_LEN = 2048
PAGE_SIZE = 128

ATTN_WIDTH = D_MODEL // 2
POOL_WIDTH = D_MODEL - ATTN_WIDTH
HEAD_DIM = 128
N_HEADS = ATTN_WIDTH // HEAD_DIM
POOL_WINDOWS = (2, 4, 8, 16)
N_POOL_GROUPS = len(POOL_WINDOWS)
POOL_GROUP = POOL_WIDTH // N_POOL_GROUPS
POOL_HIST = max(POOL_WINDOWS) - 1
IN_WIDTH = 3 * ATTN_WIDTH + N_HEADS + POOL_WIDTH
D_FF = 4 * D_MODEL
Q_BLOCK = 128
RMS_EPS = 1e-6
ATTN_SCALE = HEAD_DIM ** -0.5
N_ADA = 6

kernel_name = 'hybrid_fox_pool_adaln_step'


def rmsnorm(x, g):
    xf = x.astype(jnp.float32)
    inv = lax.rsqrt(jnp.mean(xf * xf, axis=-1, keepdims=True) + RMS_EPS)
    return (xf * inv * g.astype(jnp.float32)).astype(x.dtype)


def gather_pages(pool, page_table):
    rows = pool[page_table]
    return rows.reshape((page_table.shape[0], page_table.shape[1] * pool.shape[1]) + pool.shape[2:])


def pool_mix(u, hist, start, w_pool, pool_scale):
    b, t, _ = u.shape
    uf = u.astype(jnp.float32)
    ext = jnp.concatenate([jnp.zeros((b, 1, POOL_WIDTH), jnp.float32), hist.astype(jnp.float32), uf], axis=1)
    csum = jnp.cumsum(ext, axis=1)
    cur = csum[:, POOL_HIST + 1:]
    pos = start + jnp.arange(t)
    diffs = []
    for g, w in enumerate(POOL_WINDOWS):
        sl = slice(g * POOL_GROUP, (g + 1) * POOL_GROUP)
        lo = csum[:, POOL_HIST + 1 - w: POOL_HIST + 1 - w + t, sl]
        cnt = jnp.minimum(pos + 1, w).astype(jnp.float32)[None, :, None]
        diffs.append((cur[..., sl] - lo) / cnt - uf[..., sl])
    d = jnp.stack(diffs, axis=2)
    y = jnp.einsum('btgc,gcd->btgd', d, w_pool.astype(jnp.float32)).reshape(b, t, POOL_WIDTH)
    y = y * pool_scale.astype(jnp.float32)
    new_hist = ext[:, -POOL_HIST:].astype(u.dtype)
    return y.astype(u.dtype), new_hist


def fox_prompt(q, k, v, logf):
    b, s, h, dh = q.shape
    F = jnp.cumsum(logf, axis=1)
    Fk = F.transpose(0, 2, 1)
    nb = s // Q_BLOCK
    qb = q.reshape(b, nb, Q_BLOCK, h, dh).transpose(1, 0, 2, 3, 4)
    Fb = F.reshape(b, nb, Q_BLOCK, h).transpose(1, 0, 3, 2)
    kpos = jnp.arange(s)

    def one_block(args):
        qi, Fi, i = args
        sc = jnp.einsum('bqhd,bkhd->bhqk', qi, k).astype(jnp.float32) * ATTN_SCALE
        sc = sc + Fi[..., None] - Fk[:, :, None, :]
        qpos = i * Q_BLOCK + jnp.arange(Q_BLOCK)
        sc = jnp.where(kpos[None, :] <= qpos[:, None], sc, -jnp.inf)
        p = jax.nn.softmax(sc, axis=-1)
        return jnp.einsum('bhqk,bkhd->bqhd', p.astype(v.dtype), v)

    o = lax.map(one_block, (qb, Fb, jnp.arange(nb)))
    return o.transpose(1, 0, 2, 3, 4).reshape(b, s, h * dh)


def fox_sample(q, k, v, logf, k_past, v_past, logf_past):
    b, t, h, dh = q.shape
    p_len = k_past.shape[1]
    F = jnp.cumsum(jnp.concatenate([logf_past.astype(jnp.float32), logf], axis=1), axis=1)
    Fq = F[:, p_len:].transpose(0, 2, 1)
    Fp = F[:, :p_len].transpose(0, 2, 1)
    s_past = jnp.einsum('bqhd,bkhd->bhqk', q, k_past).astype(jnp.float32) * ATTN_SCALE
    s_past = s_past + Fq[..., None] - Fp[:, :, None, :]
    s_new = jnp.einsum('bqhd,bkhd->bhqk', q, k).astype(jnp.float32) * ATTN_SCALE
    s_new = s_new + Fq[..., None] - Fq[:, :, None, :]
    causal = jnp.arange(t)[None, :] <= jnp.arange(t)[:, None]
    s_new = jnp.where(causal, s_new, -jnp.inf)
    p = jax.nn.softmax(jnp.concatenate([s_past, s_new], axis=-1), axis=-1)
    o = (jnp.einsum('bhqk,bkhd->bqhd', p[..., :p_len].astype(v.dtype), v_past)
         + jnp.einsum('bhqk,bkhd->bqhd', p[..., p_len:].astype(v.dtype), v))
    return o.reshape(b, t, h * dh)


def layer(x, c, hist, start, past, w_ada, b_ada, g_mix, w_in, b_f, w_pool, pool_scale,
          w_out, g_ffn, w_up, w_down):
    b, t, _ = x.shape
    mod = (jax.nn.silu(c) @ w_ada + b_ada)[:, None, :]
    sh1, sc1, ga1, sh2, sc2, ga2 = jnp.split(mod, N_ADA, axis=-1)
    h = rmsnorm(x, g_mix) * (1 + sc1) + sh1
    z = h @ w_in
    q, k, v, fl, u = jnp.split(
        z, [ATTN_WIDTH, 2 * ATTN_WIDTH, 3 * ATTN_WIDTH, 3 * ATTN_WIDTH + N_HEADS], axis=-1)
    q = q.reshape(b, t, N_HEADS, HEAD_DIM)
    k = k.reshape(b, t, N_HEADS, HEAD_DIM)
    v = v.reshape(b, t, N_HEADS, HEAD_DIM)
    logf = jax.nn.log_sigmoid((fl + b_f).astype(jnp.float32))
    if past is None:
        a = fox_prompt(q, k, v, logf)
    else:
        a = fox_sample(q, k, v, logf, *past)
    pm, new_hist = pool_mix(u, hist, start, w_pool, pool_scale)
    y = jnp.concatenate([a.astype(x.dtype), pm.astype(x.dtype)], axis=-1) @ w_out
    x = x + ga1 * y
    h2 = rmsnorm(x, g_ffn) * (1 + sc2) + sh2
    x = x + ga2 * (jnp.square(jax.nn.relu(h2 @ w_up)) @ w_down)
    return x, k, v, logf, new_hist


def setup_inputs(seed: int = 0) -> dict:
    key = jax.random.key(seed)
    ks = jax.random.split(key, 24)

    def nrm(k, shape, s=1.0):
        return s * jax.random.normal(k, shape, jnp.float32)

    n_pages = PAST_LEN // PAGE_SIZE
    n_used = DEC_BATCH * n_pages
    n_phys = n_used + n_used // 4
    page_table = jax.random.permutation(ks[8], n_phys)[:n_used].reshape(DEC_BATCH, n_pages).astype(jnp.int32)
    return {
        'x_prompt': nrm(ks[0], (BATCH, SEQ, D_MODEL)),
        'x_sample': nrm(ks[1], (DEC_BATCH, DEC_SEQ, D_MODEL)),
        'c_prompt': nrm(ks[2], (BATCH, D_MODEL)),
        'c_sample': nrm(ks[3], (DEC_BATCH, D_MODEL)),
        'cache_k': nrm(ks[4], (DEPTH, n_phys, PAGE_SIZE, N_HEADS, HEAD_DIM)),
        'cache_v': nrm(ks[5], (DEPTH, n_phys, PAGE_SIZE, N_HEADS, HEAD_DIM)),
        'cache_logf': jax.nn.log_sigmoid(2.0 + nrm(ks[6], (DEPTH, n_phys, PAGE_SIZE, N_HEADS))),
        'state_pool': nrm(ks[7], (DEPTH, DEC_BATCH, POOL_HIST, POOL_WIDTH)),
        'page_table': page_table,
        'w_ada': nrm(ks[9], (DEPTH, D_MODEL, N_ADA * D_MODEL), 0.2 * D_MODEL ** -0.5),
        'b_ada': nrm(ks[10], (DEPTH, N_ADA * D_MODEL), 0.02),
        'g_mix': 1.0 + nrm(ks[11], (DEPTH, D_MODEL), 0.1),
        'w_in': nrm(ks[12], (DEPTH, D_MODEL, IN_WIDTH), D_MODEL ** -0.5),
        'b_f': 2.0 + nrm(ks[13], (DEPTH, N_HEADS), 0.5),
        'w_pool': nrm(ks[14], (DEPTH, N_POOL_GROUPS, POOL_GROUP, POOL_GROUP), POOL_GROUP ** -0.5),
        'pool_scale': 1.0 + nrm(ks[15], (DEPTH, POOL_WIDTH), 0.1),
        'w_out': nrm(ks[16], (DEPTH, D_MODEL, D_MODEL), D_MODEL ** -0.5),
        'g_ffn': 1.0 + nrm(ks[17], (DEPTH, D_MODEL), 0.1),
        'w_up': nrm(ks[18], (DEPTH, D_MODEL, D_FF), D_MODEL ** -0.5),
        'w_down': nrm(ks[19], (DEPTH, D_FF, D_MODEL), D_FF ** -0.5),
        'g_final': 1.0 + nrm(ks[20], (D_MODEL,), 0.1),
    }


def reference(x_prompt, x_sample, c_prompt, c_sample, cache_k, cache_v, cache_logf, state_pool,
              page_table, w_ada, b_ada, g_mix, w_in, b_f, w_pool, pool_scale, w_out, g_ffn,
              w_up, w_down, g_final):
    past_len = page_table.shape[1] * PAGE_SIZE
    xp, xs = x_prompt, x_sample
    kp_l, vp_l, fp_l, sp_l = [], [], [], []
    ks_l, vs_l, fs_l, ss_l = [], [], [], []
    for l in range(DEPTH):
        params = (w_ada[l], b_ada[l], g_mix[l], w_in[l], b_f[l], w_pool[l], pool_scale[l],
                  w_out[l], g_ffn[l], w_up[l], w_down[l])
        hist0 = jnp.zeros((xp.shape[0], POOL_HIST, POOL_WIDTH), xp.dtype)
        xp, kp, vp, fp, sp = layer(xp, c_prompt, hist0, 0, None, *params)
        past = (gather_pages(cache_k[l], page_table),
                gather_pages(cache_v[l], page_table),
                gather_pages(cache_logf[l], page_table))
        xs, kn, vn, fn, sn = layer(xs, c_sample, state_pool[l], past_len, past, *params)
        kp_l.append(kp); vp_l.append(vp); fp_l.append(fp); sp_l.append(sp)
        ks_l.append(kn); vs_l.append(vn); fs_l.append(fn); ss_l.append(sn)
    y_prompt = rmsnorm(xp, g_final)
    y_sample = rmsnorm(xs, g_final)
    return (y_prompt, y_sample,
            jnp.stack(kp_l), jnp.stack(vp_l), jnp.stack(fp_l), jnp.stack(sp_l),
            jnp.stack(ks_l), jnp.stack(vs_l), jnp.stack(fs_l), jnp.stack(ss_l))
```

```python
import functools

import jax
import jax.numpy as jnp
from jax import lax
from jax.experimental import pallas as pl
from jax.experimental.pallas import tpu as pltpu

F32 = jnp.float32
BF16 = jnp.bfloat16

HEAD_DIM = 128
POOL_WINDOWS = (2, 4, 8, 16)
POOL_HIST = max(POOL_WINDOWS) - 1
HALO = 16
RMS_EPS = 1e-6
N_ADA = 6
PAGE_SIZE = 128
ATTN_SCALE = HEAD_DIM ** -0.5
LANES = 128
V7X_VMEM_LIMIT = 56 * 1024 * 1024
NEG_INF = float("-inf")
NT_DIMS = (((1,), (1,)), ((), ()))


def _params(n_grid):
    return pltpu.CompilerParams(
        dimension_semantics=("arbitrary",) * n_grid, vmem_limit_bytes=V7X_VMEM_LIMIT)


def _log_sigmoid(z):
    return jnp.minimum(z, 0.0) - jnp.log1p(jnp.exp(-jnp.abs(z)))


def _rms_scale(x):
    return lax.rsqrt(jnp.mean(x * x, axis=-1, keepdims=True) + RMS_EPS)


def _mod_block(mod, tn):
    return (mod.shape[0], tn) if mod.ndim == 2 else (None, 1, tn)


def _mod_index(mod, batch_idx, col_blk):
    return (0, col_blk) if mod.ndim == 2 else (batch_idx, 0, col_blk)


def _row_groups(n_rows, mod_ref):
    group = n_rows if mod_ref.shape[0] == 1 else mod_ref.shape[0]
    return [slice(r, r + group) for r in range(0, n_rows, group)]


def _mod_kernel(c_ref, w_ref, b_ref, o_ref, s_ref):
    @pl.when(pl.program_id(0) == 0)
    def _():
        c = c_ref[...]
        s_ref[...] = (c * jax.nn.sigmoid(c)).astype(BF16)

    o_ref[...] = jnp.dot(s_ref[...], w_ref[...].astype(BF16),
                         preferred_element_type=F32) + b_ref[...]


def _modulation(c_all, w_ada, b_ada):
    rows, d = c_all.shape
    n = w_ada.shape[1]
    tn = 512
    return pl.pallas_call(
        _mod_kernel,
        grid=(n // tn,),
        in_specs=[pl.BlockSpec((rows, d), lambda j: (0, 0)),
                  pl.BlockSpec((d, tn), lambda j: (0, j)),
                  pl.BlockSpec((1, tn), lambda j: (0, j))],
        out_specs=pl.BlockSpec((rows, tn), lambda j: (0, j)),
        out_shape=jax.ShapeDtypeStruct((rows, n), F32),
        scratch_shapes=[pltpu.VMEM((rows, d), BF16)],
        compiler_params=_params(1),
        name="adaln_mod",
    )(c_all, w_ada, b_ada.reshape(1, n))


def _norm_kernel(x_ref, g_ref, sc_ref, sh_ref, o_ref):
    for rows in _row_groups(x_ref.shape[0], sc_ref):
        x = x_ref[rows, :]
        h = x * _rms_scale(x) * g_ref[...]
        o_ref[rows, :] = (h * (1.0 + sc_ref[...]) + sh_ref[...]).astype(BF16)


def _norm_mod(x, g, mod, rows_per_batch, sc_chunk, sh_chunk):
    m, d = x.shape
    tm = 256
    tiles = rows_per_batch // tm

    def mod_spec(chunk):
        return pl.BlockSpec(_mod_block(mod, d), lambda i: _mod_index(mod, i // tiles, chunk))

    return pl.pallas_call(
        _norm_kernel,
        grid=(m // tm,),
        in_specs=[pl.BlockSpec((tm, d), lambda i: (i, 0)),
                  pl.BlockSpec((1, d), lambda i: (0, 0)),
                  mod_spec(sc_chunk), mod_spec(sh_chunk)],
        out_specs=pl.BlockSpec((tm, d), lambda i: (i, 0)),
        out_shape=jax.ShapeDtypeStruct((m, d), BF16),
        compiler_params=_params(1),
        name="norm_mod",
    )(x, g.reshape(1, d), mod, mod)


def _proj_kernel(x_ref, w_ref, *rest):
    out_refs, wbf_ref = rest[:-1], rest[-1]

    @pl.when(pl.program_id(1) == 0)
    def _():
        wbf_ref[...] = w_ref[...].astype(BF16)

    acc = jnp.dot(x_ref[...], wbf_ref[...], preferred_element_type=F32)
    for o_ref in out_refs:
        o_ref[...] = acc.astype(o_ref.dtype)


def _proj(x, w, col0, n, out_dtypes, name):
    m, k = x.shape
    tm, tn = 512, 512
    off = col0 // tn
    return pl.pallas_call(
        _proj_kernel,
        grid=(n // tn, m // tm),
        in_specs=[pl.BlockSpec((tm, k), lambda j, i: (i, 0)),
                  pl.BlockSpec((k, tn), lambda j, i: (0, off + j))],
        out_specs=[pl.BlockSpec((tm, tn), lambda j, i: (i, j)) for _ in out_dtypes],
        out_shape=[jax.ShapeDtypeStruct((m, n), dt) for dt in out_dtypes],
        scratch_shapes=[pltpu.VMEM((k, tn), BF16)],
        compiler_params=_params(2),
        name=name,
    )(x, w)


def _logf_prompt_kernel(h_ref, w_ref, b_ref, lf_ref, cum_ref, carry_ref, *, tm, n_heads):
    @pl.when(pl.program_id(1) == 0)
    def _():
        carry_ref[...] = jnp.zeros_like(carry_ref)

    lf = _log_sigmoid(jnp.dot(h_ref[...], w_ref[...], preferred_element_type=F32) + b_ref[...])
    row = lax.broadcasted_iota(jnp.int32, (tm, tm), 0)
    col = lax.broadcasted_iota(jnp.int32, (tm, tm), 1)
    tri = (col <= row).astype(F32)
    cum = jnp.dot(tri, lf, preferred_element_type=F32,
                  precision=lax.Precision.HIGHEST) + carry_ref[...]
    carry_ref[...] = cum[tm - 1:tm, :]
    lf_ref[...] = lf[:, :n_heads]
    cum_ref[...] = cum[:, :n_heads]


def _logf_prompt(h, w_f, b_f, seq, n_heads):
    m, d = h.shape
    tm = 512
    tiles = seq // tm
    return pl.pallas_call(
        functools.partial(_logf_prompt_kernel, tm=tm, n_heads=n_heads),
        grid=(m // seq, tiles),
        in_specs=[pl.BlockSpec((tm, d), lambda b, t: (b * tiles + t, 0)),
                  pl.BlockSpec((d, LANES), lambda b, t: (0, 0)),
                  pl.BlockSpec((1, LANES), lambda b, t: (0, 0))],
        out_specs=[pl.BlockSpec((tm, n_heads), lambda b, t: (b * tiles + t, 0))] * 2,
        out_shape=[jax.ShapeDtypeStruct((m, n_heads), F32)] * 2,
        scratch_shapes=[pltpu.VMEM((1, LANES), F32)],
        compiler_params=_params(2),
        name="logf_prompt",
    )(h, w_f, b_f)


def _logf_sample_kernel(h_ref, w_ref, b_ref, lf_ref, cum_ref, *, rows, n_seq, n_heads):
    lf = _log_sigmoid(jnp.dot(h_ref[...], w_ref[...], preferred_element_type=F32) + b_ref[...])
    row = lax.broadcasted_iota(jnp.int32, (rows, rows), 0)
    col = lax.broadcasted_iota(jnp.int32, (rows, rows), 1)
    tri = jnp.logical_and((row % n_seq) == (col % n_seq), col <= row).astype(F32)
    cum = jnp.dot(tri, lf, preferred_element_type=F32, precision=lax.Precision.HIGHEST)
    lf_ref[...] = lf[:, :n_heads]
    cum_ref[...] = cum[:, :n_heads]


def _logf_sample(h, w_f, b_f, n_seq, n_heads):
    rows, d = h.shape
    return pl.pallas_call(
        functools.partial(_logf_sample_kernel, rows=rows, n_seq=n_seq, n_heads=n_heads),
        grid=(1,),
        in_specs=[pl.BlockSpec((rows, d), lambda i: (0, 0)),
                  pl.BlockSpec((d, LANES), lambda i: (0, 0)),
                  pl.BlockSpec((1, LANES), lambda i: (0, 0))],
        out_specs=[pl.BlockSpec((rows, n_heads), lambda i: (0, 0))] * 2,
        out_shape=[jax.ShapeDtypeStruct((rows, n_heads), F32)] * 2,
        compiler_params=_params(1),
        name="logf_sample",
    )(h, w_f, b_f)


def _past_bias_kernel(pt_ref, *refs, n_pages):
    page_refs, o_ref = refs[:n_pages], refs[n_pages]
    row = lax.broadcasted_iota(jnp.int32, (PAGE_SIZE, PAGE_SIZE), 0)
    col = lax.broadcasted_iota(jnp.int32, (PAGE_SIZE, PAGE_SIZE), 1)
    later = (col > row).astype(F32)
    carry = jnp.zeros((1, page_refs[0].shape[-1]), F32)
    for p in reversed(range(n_pages)):
        x = page_refs[p][...]
        o_ref[p] = jnp.dot(later, x, preferred_element_type=F32,
                           precision=lax.Precision.HIGHEST) + carry
        carry = carry + jnp.sum(x, axis=0, keepdims=True)


def _past_bias(page_table, logf_pages):
    n_seq, n_pages = page_table.shape
    _, page, n_heads = logf_pages.shape

    def page_spec(p):
        return pl.BlockSpec((None, page, n_heads), lambda b, pt: (pt[b, p], 0, 0))

    return pl.pallas_call(
        functools.partial(_past_bias_kernel, n_pages=n_pages),
        grid_spec=pltpu.PrefetchScalarGridSpec(
            num_scalar_prefetch=1,
            grid=(n_seq,),
            in_specs=[page_spec(p) for p in range(n_pages)],
            out_specs=pl.BlockSpec((None, n_pages, page, n_heads), lambda b, pt: (b, 0, 0, 0)),
        ),
        out_shape=jax.ShapeDtypeStruct((n_seq, n_pages, page, n_heads), F32),
        compiler_params=_params(1),
        name="past_bias",
    )(page_table, *([logf_pages] * n_pages))


def _pool_prompt_kernel(u_ref, halo_ref, wp_ref, ps_ref, o_ref, ext_ref, *, tm, group):
    t = pl.program_id(1)
    ext_ref[0:HALO, :] = jnp.where(t > 0, halo_ref[...], 0.0)
    ext_ref[HALO:HALO + tm, :] = u_ref[...]
    pos = t * tm + lax.broadcasted_iota(jnp.int32, (tm, 1), 0)
    for g, w in enumerate(POOL_WINDOWS):
        cs = slice(g * group, (g + 1) * group)
        cur = u_ref[:, cs]
        win = cur
        for j in range(1, w):
            win = win + ext_ref[HALO - j:HALO - j + tm, cs]
        cnt = jnp.minimum(pos + 1, w).astype(F32)
        d = win / cnt - cur
        y = jnp.dot(d.astype(BF16), wp_ref[g], preferred_element_type=F32) * ps_ref[:, cs]
        o_ref[:, cs] = y.astype(BF16)


def _pool_prompt(u, w_pool_bf, pool_scale, seq):
    m, width = u.shape
    n_groups, group, _ = w_pool_bf.shape
    tm = 256
    tiles = seq // tm
    return pl.pallas_call(
        functools.partial(_pool_prompt_kernel, tm=tm, group=group),
        grid=(m // seq, tiles),
        in_specs=[pl.BlockSpec((tm, width), lambda b, t: (b * tiles + t, 0)),
                  pl.BlockSpec((HALO, width),
                               lambda b, t: (jnp.maximum((b * tiles + t) * (tm // HALO) - 1, 0), 0)),
                  pl.BlockSpec((n_groups, group, group), lambda b, t: (0, 0, 0)),
                  pl.BlockSpec((1, width), lambda b, t: (0, 0))],
        out_specs=pl.BlockSpec((tm, width), lambda b, t: (b * tiles + t, 0)),
        out_shape=jax.ShapeDtypeStruct((m, width), BF16),
        scratch_shapes=[pltpu.VMEM((HALO + tm, width), F32)],
        compiler_params=_params(2),
        name="pool_prompt",
    )(u, u, w_pool_bf, pool_scale.reshape(1, width))


def _pool_sample_kernel(ext_ref, wp_ref, ps_ref, o_ref, *, t_new, n_seq):
    g = pl.program_id(0)
    for gi, w in enumerate(POOL_WINDOWS):
        @pl.when(g == gi)
        def _(w=w):
            ds = []
            for t in range(t_new):
                cur = ext_ref[POOL_HIST + t]
                win = cur
                for j in range(1, w):
                    win = win + ext_ref[POOL_HIST + t - j]
                ds.append(win / float(w) - cur)
            d = jnp.concatenate(ds, axis=0).astype(BF16)
            y = jnp.dot(d, wp_ref[...], preferred_element_type=F32) * ps_ref[...]
            for t in range(t_new):
                o_ref[t] = y[t * n_seq:(t + 1) * n_seq].astype(BF16)


def _pool_sample(ext_t, w_pool_bf, pool_scale, t_new):
    rows, n_seq, width = ext_t.shape
    n_groups, group, _ = w_pool_bf.shape
    return pl.pallas_call(
        functools.partial(_pool_sample_kernel, t_new=t_new, n_seq=n_seq),
        grid=(n_groups,),
        in_specs=[pl.BlockSpec((rows, n_seq, group), lambda g: (0, 0, g)),
                  pl.BlockSpec((None, group, group), lambda g: (g, 0, 0)),
                  pl.BlockSpec((1, group), lambda g: (0, g))],
        out_specs=pl.BlockSpec((t_new, n_seq, group), lambda g: (0, 0, g)),
        out_shape=jax.ShapeDtypeStruct((t_new, n_seq, width), BF16),
        compiler_params=_params(1),
        name="pool_sample",
    )(ext_t, w_pool_bf, pool_scale.reshape(1, width))


def _softmax_step(s, v, m_ref, l_ref, acc_ref):
    m_prev = m_ref[...]
    m_new = jnp.maximum(m_prev, jnp.max(s, axis=-1, keepdims=True))
    alpha = jnp.exp(m_prev - m_new)
    p = jnp.exp(s - m_new)
    l_ref[...] = alpha * l_ref[...] + jnp.sum(p, axis=-1, keepdims=True)
    acc_ref[...] = alpha * acc_ref[...] + jnp.dot(p.astype(BF16), v, preferred_element_type=F32)
    m_ref[...] = m_new


def _attn_prompt_kernel(q_ref, k_ref, v_ref, frow_ref, fcol_ref, o_ref,
                        m_ref, l_ref, acc_ref, *, seq, tq, n_heads):
    h = pl.program_id(1)
    lane = lax.broadcasted_iota(jnp.int32, (seq, n_heads), 1)
    fcol_all = jnp.sum(jnp.where(lane == h, fcol_ref[...], 0.0), axis=1, keepdims=True)
    row = lax.broadcasted_iota(jnp.int32, (tq, tq), 0)
    col = lax.broadcasted_iota(jnp.int32, (tq, tq), 1)
    causal = col <= row

    def scores(q, fcol, kj):
        start = pl.multiple_of(kj * tq, tq)
        k = k_ref[pl.ds(start, tq), :]
        s = lax.dot_general(q, k, NT_DIMS, preferred_element_type=F32)
        frow = frow_ref[pl.ds(h, 1), pl.ds(start, tq)]
        return s * ATTN_SCALE + fcol - frow, v_ref[pl.ds(start, tq), :]

    for qi in range(seq // tq):
        q = q_ref[qi * tq:(qi + 1) * tq, :]
        fcol = fcol_all[qi * tq:(qi + 1) * tq]
        m_ref[...] = jnp.full_like(m_ref, NEG_INF)
        l_ref[...] = jnp.zeros_like(l_ref)
        acc_ref[...] = jnp.zeros_like(acc_ref)

        def body(kj, carry, q=q, fcol=fcol):
            s, v = scores(q, fcol, kj)
            _softmax_step(s, v, m_ref, l_ref, acc_ref)
            return carry

        lax.fori_loop(0, qi, body, 0)
        s, v = scores(q, fcol, qi)
        _softmax_step(jnp.where(causal, s, NEG_INF), v, m_ref, l_ref, acc_ref)
        o_ref[qi * tq:(qi + 1) * tq, :] = (acc_ref[...] / l_ref[...]).astype(BF16)


def _attn_prompt(q, k, v, cum_t, cum, seq, n_heads):
    m, width = q.shape
    tq = 512
    blk = pl.BlockSpec((seq, HEAD_DIM), lambda b, h: (b, h))
    return pl.pallas_call(
        functools.partial(_attn_prompt_kernel, seq=seq, tq=tq, n_heads=n_heads),
        grid=(m // seq, n_heads),
        in_specs=[blk, blk, blk,
                  pl.BlockSpec((None, n_heads, seq), lambda b, h: (b, 0, 0)),
                  pl.BlockSpec((seq, n_heads), lambda b, h: (b, 0))],
        out_specs=blk,
        out_shape=jax.ShapeDtypeStruct((m, width), BF16),
        scratch_shapes=[pltpu.VMEM((tq, 1), F32), pltpu.VMEM((tq, 1), F32),
                        pltpu.VMEM((tq, HEAD_DIM), F32)],
        compiler_params=_params(2),
        name="attn_prompt",
    )(q, k, v, cum_t, cum)


def _attn_sample_kernel(pt_ref, q_ref, *refs, n_pp, t_new, n_heads):
    k_refs = refs[:n_pp]
    v_refs = refs[n_pp:2 * n_pp]
    pb_refs = refs[2 * n_pp:3 * n_pp]
    kn_ref, vn_ref, cnrow_ref, cncol_ref, o_ref = refs[3 * n_pp:3 * n_pp + 5]
    mb_ref, m_ref, l_ref, acc_ref, knp_ref, vnp_ref = refs[3 * n_pp + 5:]
    c = pl.program_id(1)
    n_rows = n_heads * t_new
    n_cols = PAGE_SIZE * n_heads

    @pl.when(c == 0)
    def _():
        row = lax.broadcasted_iota(jnp.int32, (n_rows, n_cols), 0)
        col = lax.broadcasted_iota(jnp.int32, (n_rows, n_cols), 1)
        same_head = (row // t_new) == (col % n_heads)
        mb_ref[...] = jnp.where(same_head, cncol_ref[...], NEG_INF)
        m_ref[...] = jnp.full_like(m_ref, NEG_INF)
        l_ref[...] = jnp.zeros_like(l_ref)
        acc_ref[...] = jnp.zeros_like(acc_ref)

    q = q_ref[...]
    s_parts = []
    for i in range(n_pp):
        kp = k_refs[i][...].astype(BF16)
        s = lax.dot_general(q, kp, NT_DIMS, preferred_element_type=F32)
        s_parts.append(s * ATTN_SCALE + mb_ref[...] + pb_refs[i][...])
    m_prev = m_ref[...]
    m_new = m_prev
    for s in s_parts:
        m_new = jnp.maximum(m_new, jnp.max(s, axis=-1, keepdims=True))
    alpha = jnp.exp(m_prev - m_new)
    l_new = alpha * l_ref[...]
    acc = alpha * acc_ref[...]
    for i, s in enumerate(s_parts):
        p = jnp.exp(s - m_new)
        l_new = l_new + jnp.sum(p, axis=-1, keepdims=True)
        acc = acc + jnp.dot(p.astype(BF16), v_refs[i][...].astype(BF16),
                            preferred_element_type=F32)
    m_ref[...] = m_new
    l_ref[...] = l_new
    acc_ref[...] = acc

    @pl.when(c == pl.num_programs(1) - 1)
    def _():
        n_new = t_new * n_heads
        knp_ref[0:n_new, :] = kn_ref[...].astype(BF16)
        knp_ref[n_new:, :] = jnp.zeros((LANES - n_new, HEAD_DIM), BF16)
        vnp_ref[0:n_new, :] = vn_ref[...].astype(BF16)
        vnp_ref[n_new:, :] = jnp.zeros((LANES - n_new, HEAD_DIM), BF16)
        s = lax.dot_general(q, knp_ref[...], NT_DIMS, preferred_element_type=F32)
        row = lax.broadcasted_iota(jnp.int32, (n_rows, LANES), 0)
        col = lax.broadcasted_iota(jnp.int32, (n_rows, LANES), 1)
        valid = jnp.logical_and((row // t_new) == (col % n_heads),
                                (col // n_heads) <= (row % t_new))
        s = jnp.where(valid, s * ATTN_SCALE + cncol_ref[...] - cnrow_ref[...], NEG_INF)
        _softmax_step(s, vnp_ref[...], m_ref, l_ref, acc_ref)
        o_ref[...] = (acc_ref[...] / l_ref[...]).astype(BF16)


def _attn_sample(page_table, q_rows, k_pages, v_pages, past_bias, k_new, v_new,
                 cn_row, cn_col, t_new, n_heads):
    n_seq, n_pages = page_table.shape
    n_rows = n_heads * t_new
    n_cols = PAGE_SIZE * n_heads
    n_new = t_new * n_heads
    n_pp = 4
    steps = n_pages // n_pp

    def page_spec(i):
        return pl.BlockSpec((None, n_cols, HEAD_DIM),
                            lambda b, c, pt: (pt[b, c * n_pp + i], 0, 0))

    def bias_spec(i):
        return pl.BlockSpec((None, None, 1, n_cols), lambda b, c, pt: (b, c * n_pp + i, 0, 0))

    def seq_spec(r, w):
        return pl.BlockSpec((None, r, w), lambda b, c, pt: (b, 0, 0))

    return pl.pallas_call(
        functools.partial(_attn_sample_kernel, n_pp=n_pp, t_new=t_new, n_heads=n_heads),
        grid_spec=pltpu.PrefetchScalarGridSpec(
            num_scalar_prefetch=1,
            grid=(n_seq, steps),
            in_specs=([seq_spec(n_rows, HEAD_DIM)]
                      + [page_spec(i) for i in range(n_pp)] * 2
                      + [bias_spec(i) for i in range(n_pp)]
                      + [seq_spec(n_new, HEAD_DIM), seq_spec(n_new, HEAD_DIM),
                         seq_spec(1, LANES), seq_spec(n_rows, 1)]),
            out_specs=seq_spec(n_rows, HEAD_DIM),
            scratch_shapes=[pltpu.VMEM((n_rows, n_cols), F32),
                            pltpu.VMEM((n_rows, 1), F32), pltpu.VMEM((n_rows, 1), F32),
                            pltpu.VMEM((n_rows, HEAD_DIM), F32),
                            pltpu.VMEM((LANES, HEAD_DIM), BF16),
                            pltpu.VMEM((LANES, HEAD_DIM), BF16)],
        ),
        out_shape=jax.ShapeDtypeStruct((n_seq, n_rows, HEAD_DIM), BF16),
        compiler_params=_params(2),
        name="attn_sample",
    )(page_table, q_rows, *([k_pages] * n_pp), *([v_pages] * n_pp), *([past_bias] * n_pp),
      k_new, v_new, cn_row, cn_col)


def _outproj_kernel(a_ref, p_ref, wa_ref, wp_ref, x_ref, ga_ref, o_ref, wabf_ref, wpbf_ref):
    @pl.when(pl.program_id(1) == 0)
    def _():
        wabf_ref[...] = wa_ref[...].astype(BF16)
        wpbf_ref[...] = wp_ref[...].astype(BF16)

    y = jnp.dot(a_ref[...], wabf_ref[...], preferred_element_type=F32)
    y = y + jnp.dot(p_ref[...], wpbf_ref[...], preferred_element_type=F32)
    for rows in _row_groups(x_ref.shape[0], ga_ref):
        o_ref[rows, :] = x_ref[rows, :] + ga_ref[...] * y[rows, :]


def _outproj(a, pm, w_out, x, mod, rows_per_batch, ga_chunk):
    m, ka = a.shape
    kp = pm.shape[1]
    d = w_out.shape[1]
    tm, tn = 512, 512
    tiles = rows_per_batch // tm
    ga_col0 = ga_chunk * (d // tn)
    return pl.pallas_call(
        _outproj_kernel,
        grid=(d // tn, m // tm),
        in_specs=[pl.BlockSpec((tm, ka), lambda j, i: (i, 0)),
                  pl.BlockSpec((tm, kp), lambda j, i: (i, 0)),
                  pl.BlockSpec((ka, tn), lambda j, i: (0, j)),
                  pl.BlockSpec((kp, tn), lambda j, i: (ka // kp, j)),
                  pl.BlockSpec((tm, tn), lambda j, i: (i, j)),
                  pl.BlockSpec(_mod_block(mod, tn),
                               lambda j, i: _mod_index(mod, i // tiles, ga_col0 + j))],
        out_specs=pl.BlockSpec((tm, tn), lambda j, i: (i, j)),
        out_shape=jax.ShapeDtypeStruct((m, d), F32),
        scratch_shapes=[pltpu.VMEM((ka, tn), BF16), pltpu.VMEM((kp, tn), BF16)],
        compiler_params=_params(2),
        name="out_proj",
    )(a, pm, w_out, w_out, x, mod)


def _ffn_kernel(x_ref, g_ref, sc_ref, sh_ref, ga_ref, gf_ref, wu_ref, wd_ref, o_ref, h_ref,
                *, tn):
    f = pl.program_id(1)
    groups = _row_groups(x_ref.shape[0], sc_ref)

    @pl.when(f == 0)
    def _():
        for rows in groups:
            x = x_ref[rows, :]
            h = x * _rms_scale(x) * g_ref[...]
            h_ref[rows, :] = (h * (1.0 + sc_ref[...]) + sh_ref[...]).astype(BF16)
        o_ref[...] = jnp.zeros_like(o_ref)

    hid = jnp.dot(h_ref[...], wu_ref[...], preferred_element_type=F32)
    hid = jnp.square(jnp.maximum(hid, 0.0)).astype(BF16)
    for j in range(o_ref.shape[1] // tn):
        cs = slice(j * tn, (j + 1) * tn)
        o_ref[:, cs] += jnp.dot(hid, wd_ref[:, cs], preferred_element_type=F32)

    @pl.when(f == pl.num_programs(1) - 1)
    def _():
        for rows in groups:
            xo = x_ref[rows, :] + ga_ref[...] * o_ref[rows, :]
            o_ref[rows, :] = xo * _rms_scale(xo) * gf_ref[...]


def _ffn(x, g_ffn, g_final, w_up_bf, w_down_bf, mod, rows_per_batch, sc_chunk, sh_chunk, ga_chunk):
    m, d = x.shape
    d_ff = w_up_bf.shape[1]
    tm, tf = 512, 256
    tiles = rows_per_batch // tm

    def mod_spec(chunk):
        return pl.BlockSpec(_mod_block(mod, d), lambda i, f: _mod_index(mod, i // tiles, chunk))

    vec = pl.BlockSpec((1, d), lambda i, f: (0, 0))
    return pl.pallas_call(
        functools.partial(_ffn_kernel, tn=512),
        grid=(m // tm, d_ff // tf),
        in_specs=[pl.BlockSpec((tm, d), lambda i, f: (i, 0)),
                  vec, mod_spec(sc_chunk), mod_spec(sh_chunk), mod_spec(ga_chunk), vec,
                  pl.BlockSpec((d, tf), lambda i, f: (0, f)),
                  pl.BlockSpec((tf, d), lambda i, f: (f, 0))],
        out_specs=pl.BlockSpec((tm, d), lambda i, f: (i, 0)),
        out_shape=jax.ShapeDtypeStruct((m, d), F32),
        scratch_shapes=[pltpu.VMEM((tm, d), BF16)],
        compiler_params=_params(2),
        name="ffn",
    )(x, g_ffn.reshape(1, d), mod, mod, mod, g_final.reshape(1, d), w_up_bf, w_down_bf)


def _pad_cols(w, n):
    return jnp.pad(w, ((0, 0), (0, n - w.shape[1])))


def kernel(x_prompt, x_sample, c_prompt, c_sample, cache_k, cache_v, cache_logf, state_pool,
           page_table, w_ada, b_ada, g_mix, w_in, b_f, w_pool, pool_scale, w_out, g_ffn,
           w_up, w_down, g_final):
    n_b, seq, d = x_prompt.shape
    n_s, t_new, _ = x_sample.shape
    assert w_ada.shape[0] == 1, "single trunk layer"
    l = 0
    n_heads = b_f.shape[1]
    attn_w = n_heads * HEAD_DIM
    pool_w = pool_scale.shape[1]
    n_phys = cache_k.shape[1]
    n_pages = page_table.shape[1]
    n_cols = PAGE_SIZE * n_heads
    SH1, SC1, GA1, SH2, SC2, GA2 = range(N_ADA)

    w_in_l = w_in[l]
    w_f = _pad_cols(w_in_l[:, 3 * attn_w:3 * attn_w + n_heads], LANES).astype(BF16)
    b_f_p = _pad_cols(b_f[l].reshape(1, n_heads), LANES)
    w_u = w_in_l[:, 3 * attn_w + n_heads:]
    w_pool_bf = w_pool[l].astype(BF16)
    w_up_bf = w_up[l].astype(BF16)
    w_down_bf = w_down[l].astype(BF16)

    n_c = n_b + n_s
    c_rows = -(-n_c // 16) * 16
    c_all = jnp.pad(jnp.concatenate([c_prompt, c_sample], axis=0), ((0, c_rows - n_c), (0, 0)))
    mod = _modulation(c_all, w_ada[l], b_ada[l])
    mod_p = mod[:n_b].reshape(n_b, 1, N_ADA * d)
    mod_s = mod[n_b:n_c]

    def trunk_front(x2, mod_x, rows_per_batch):
        h = _norm_mod(x2, g_mix[l], mod_x, rows_per_batch, SC1, SH1)
        (q,) = _proj(h, w_in_l, 0, attn_w, [BF16], "proj_q")
        k, k_bf = _proj(h, w_in_l, attn_w, attn_w, [F32, BF16], "proj_k")
        v, v_bf = _proj(h, w_in_l, 2 * attn_w, attn_w, [F32, BF16], "proj_v")
        (u,) = _proj(h, w_u, 0, pool_w, [F32], "proj_u")
        return h, q, k, k_bf, v, v_bf, u

    def trunk_back(a, pm, x2, mod_x, rows_per_batch):
        x1 = _outproj(a, pm, w_out[l], x2, mod_x, rows_per_batch, GA1)
        return _ffn(x1, g_ffn[l], g_final, w_up_bf, w_down_bf, mod_x, rows_per_batch,
                    SC2, SH2, GA2)

    xp = x_prompt.reshape(n_b * seq, d)
    h, q, k_p, k_bf, v_p, v_bf, u_p = trunk_front(xp, mod_p, seq)
    logf_p, cum_p = _logf_prompt(h, w_f, b_f_p, seq, n_heads)
    cum_t = cum_p.reshape(n_b, seq, n_heads).transpose(0, 2, 1)
    a_p = _attn_prompt(q, k_bf, v_bf, cum_t, cum_p, seq, n_heads)
    pm_p = _pool_prompt(u_p, w_pool_bf, pool_scale[l], seq)
    y_p = trunk_back(a_p, pm_p, xp, mod_p, seq)
    hist_p = u_p.reshape(n_b, seq, pool_w)[:, seq - POOL_HIST:]

    def seq_major(x2):
        return x2.reshape(t_new, n_s, x2.shape[-1]).transpose(1, 0, 2)

    xs = x_sample.transpose(1, 0, 2).reshape(t_new * n_s, d)
    h, q, k_s, _, v_s, _, u_s = trunk_front(xs, mod_s, t_new * n_s)
    logf_s, cn = _logf_sample(h, w_f, b_f_p, n_s, n_heads)
    k_s, v_s, logf_s, cn = seq_major(k_s), seq_major(v_s), seq_major(logf_s), seq_major(cn)

    past_bias = _past_bias(page_table, cache_logf[l]).reshape(n_s, n_pages, 1, n_cols)
    cn_row = _pad_cols(cn.reshape(n_s, t_new * n_heads), LANES).reshape(n_s, 1, LANES)
    cn_col = cn.transpose(0, 2, 1).reshape(n_s, n_heads * t_new, 1)
    q_rows = q.reshape(t_new, n_s, n_heads, HEAD_DIM).transpose(1, 2, 0, 3)
    q_rows = q_rows.reshape(n_s, n_heads * t_new, HEAD_DIM)
    a_s = _attn_sample(page_table, q_rows,
                       cache_k[l].reshape(n_phys, n_cols, HEAD_DIM),
                       cache_v[l].reshape(n_phys, n_cols, HEAD_DIM),
                       past_bias,
                       k_s.reshape(n_s, t_new * n_heads, HEAD_DIM),
                       v_s.reshape(n_s, t_new * n_heads, HEAD_DIM),
                       cn_row, cn_col, t_new, n_heads)
    a_s = a_s.reshape(n_s, n_heads, t_new, HEAD_DIM).transpose(2, 0, 1, 3)
    a_s = a_s.reshape(t_new * n_s, attn_w)

    ext_t = jnp.concatenate([state_pool[l].transpose(1, 0, 2),
                             u_s.reshape(t_new, n_s, pool_w)], axis=0)
    pm_s = _pool_sample(ext_t, w_pool_bf, pool_scale[l], t_new).reshape(t_new * n_s, pool_w)
    y_s = trunk_back(a_s, pm_s, xs, mod_s, t_new * n_s)
    hist_s = ext_t[t_new:].transpose(1, 0, 2)

    kv_p = (1, n_b, seq, n_heads, HEAD_DIM)
    kv_s = (1, n_s, t_new, n_heads, HEAD_DIM)
    return (y_p.reshape(n_b, seq, d), seq_major(y_s),
            k_p.reshape(kv_p), v_p.reshape(kv_p), logf_p.reshape(1, n_b, seq, n_heads), hist_p[None],
            k_s.reshape(kv_s), v_s.reshape(kv_s), logf_s[None], hist_s[None])
```

```python
import functools

import jax
import jax.numpy as jnp
from jax import lax
from jax.experimental import pallas as pl
from jax.experimental.pallas import tpu as pltpu

F32 = jnp.float32
BF16 = jnp.bfloat16

HEAD_DIM = 128
POOL_WINDOWS = (2, 4, 8, 16)
POOL_HIST = max(POOL_WINDOWS) - 1
HALO = 16
RMS_EPS = 1e-6
N_ADA = 6
PAGE_SIZE = 128
ATTN_SCALE = HEAD_DIM ** -0.5
LANES = 128
V7X_VMEM_LIMIT = 56 * 1024 * 1024
NEG_INF = float("-inf")
LOG2E = 1.4426950408889634
NT_DIMS = (((1,), (1,)), ((), ()))


def _params(n_grid):
    return pltpu.CompilerParams(
        dimension_semantics=("arbitrary",) * n_grid, vmem_limit_bytes=V7X_VMEM_LIMIT)


def _log_sigmoid(z):
    return jnp.minimum(z, 0.0) - jnp.log1p(jnp.exp(-jnp.abs(z)))


def _rms_scale(x):
    return lax.rsqrt(jnp.mean(x * x, axis=-1, keepdims=True) + RMS_EPS)


def _mod_block(mod, tn, n_seq):
    return (n_seq, tn) if mod.ndim == 2 else (None, 1, tn)


def _mod_index(mod, batch_idx, col_blk):
    return (0, col_blk) if mod.ndim == 2 else (batch_idx, 0, col_blk)


def _row_groups(n_rows, mod_ref):
    group = n_rows if mod_ref.shape[0] == 1 else mod_ref.shape[0]
    return [slice(r, r + group) for r in range(0, n_rows, group)]


def _mod_kernel(c_ref, w_ref, b_ref, o_ref, s_ref):
    @pl.when(pl.program_id(0) == 0)
    def _():
        c = c_ref[...]
        s_ref[...] = (c * jax.nn.sigmoid(c)).astype(BF16)

    o_ref[...] = jnp.dot(s_ref[...], w_ref[...].astype(BF16),
                         preferred_element_type=F32) + b_ref[...]


def _modulation(c_all, w_ada, b_ada):
    rows, d = c_all.shape
    n = w_ada.shape[1]
    tn = 512
    return pl.pallas_call(
        _mod_kernel,
        grid=(n // tn,),
        in_specs=[pl.BlockSpec((rows, d), lambda j: (0, 0)),
                  pl.BlockSpec((d, tn), lambda j: (0, j)),
                  pl.BlockSpec((1, tn), lambda j: (0, j))],
        out_specs=pl.BlockSpec((rows, tn), lambda j: (0, j)),
        out_shape=jax.ShapeDtypeStruct((rows, n), F32),
        scratch_shapes=[pltpu.VMEM((rows, d), BF16)],
        compiler_params=_params(1),
        name="adaln_mod",
    )(c_all, w_ada, b_ada.reshape(1, n))


def _norm_kernel(x_ref, g_ref, sc_ref, sh_ref, o_ref):
    for rows in _row_groups(x_ref.shape[0], sc_ref):
        x = x_ref[rows, :]
        h = x * _rms_scale(x) * g_ref[...]
        o_ref[rows, :] = (h * (1.0 + sc_ref[...]) + sh_ref[...]).astype(BF16)


def _norm_mod(x, g, mod, rows_per_batch, n_seq, sc_chunk, sh_chunk):
    m, d = x.shape
    tm = 256
    tiles = rows_per_batch // tm

    def mod_spec(chunk):
        return pl.BlockSpec(_mod_block(mod, d, n_seq),
                            lambda i: _mod_index(mod, i // tiles, chunk))

    return pl.pallas_call(
        _norm_kernel,
        grid=(m // tm,),
        in_specs=[pl.BlockSpec((tm, d), lambda i: (i, 0)),
                  pl.BlockSpec((1, d), lambda i: (0, 0)),
                  mod_spec(sc_chunk), mod_spec(sh_chunk)],
        out_specs=pl.BlockSpec((tm, d), lambda i: (i, 0)),
        out_shape=jax.ShapeDtypeStruct((m, d), BF16),
        compiler_params=_params(1),
        name="norm_mod",
    )(x, g.reshape(1, d), mod, mod)


def _final_norm_kernel(x_ref, g_ref, o_ref):
    x = x_ref[...]
    o_ref[...] = x * _rms_scale(x) * g_ref[...]


def _final_norm(x, g):
    m, d = x.shape
    tm = 256
    return pl.pallas_call(
        _final_norm_kernel,
        grid=(m // tm,),
        in_specs=[pl.BlockSpec((tm, d), lambda i: (i, 0)),
                  pl.BlockSpec((1, d), lambda i: (0, 0))],
        out_specs=pl.BlockSpec((tm, d), lambda i: (i, 0)),
        out_shape=jax.ShapeDtypeStruct((m, d), F32),
        compiler_params=_params(1),
        name="final_norm",
    )(x, g.reshape(1, d))


def _proj_kernel(x_ref, w_ref, *rest):
    out_refs, wbf_ref = rest[:-1], rest[-1]

    @pl.when(pl.program_id(1) == 0)
    def _():
        wbf_ref[...] = w_ref[...].astype(BF16)

    acc = jnp.dot(x_ref[...], wbf_ref[...], preferred_element_type=F32)
    for o_ref in out_refs:
        o_ref[...] = acc.astype(o_ref.dtype)


def _proj(x, w, col0, n, out_dtypes, name):
    m, k = x.shape
    tm, tn = 512, 512
    off = col0 // tn
    return pl.pallas_call(
        _proj_kernel,
        grid=(n // tn, m // tm),
        in_specs=[pl.BlockSpec((tm, k), lambda j, i: (i, 0)),
                  pl.BlockSpec((k, tn), lambda j, i: (0, off + j))],
        out_specs=[pl.BlockSpec((tm, tn), lambda j, i: (i, j)) for _ in out_dtypes],
        out_shape=[jax.ShapeDtypeStruct((m, n), dt) for dt in out_dtypes],
        scratch_shapes=[pltpu.VMEM((k, tn), BF16)],
        compiler_params=_params(2),
        name=name,
    )(x, w)


def _logf_prompt_kernel(h_ref, w_ref, b_ref, lf_ref, cum_ref, carry_ref, *, tm, n_heads):
    @pl.when(pl.program_id(1) == 0)
    def _():
        carry_ref[...] = jnp.zeros_like(carry_ref)

    lf = _log_sigmoid(jnp.dot(h_ref[...], w_ref[...], preferred_element_type=F32) + b_ref[...])
    row = lax.broadcasted_iota(jnp.int32, (tm, tm), 0)
    col = lax.broadcasted_iota(jnp.int32, (tm, tm), 1)
    tri = (col <= row).astype(F32)
    cum = jnp.dot(tri, lf, preferred_element_type=F32,
                  precision=lax.Precision.HIGHEST) + carry_ref[...]
    carry_ref[...] = cum[tm - 1:tm, :]
    lf_ref[...] = lf[:, :n_heads]
    cum_ref[...] = cum[:, :n_heads]


def _logf_prompt(h, w_f, b_f, seq, n_heads):
    m, d = h.shape
    tm = 512
    tiles = seq // tm
    return pl.pallas_call(
        functools.partial(_logf_prompt_kernel, tm=tm, n_heads=n_heads),
        grid=(m // seq, tiles),
        in_specs=[pl.BlockSpec((tm, d), lambda b, t: (b * tiles + t, 0)),
                  pl.BlockSpec((d, LANES), lambda b, t: (0, 0)),
                  pl.BlockSpec((1, LANES), lambda b, t: (0, 0))],
        out_specs=[pl.BlockSpec((tm, n_heads), lambda b, t: (b * tiles + t, 0))] * 2,
        out_shape=[jax.ShapeDtypeStruct((m, n_heads), F32)] * 2,
        scratch_shapes=[pltpu.VMEM((1, LANES), F32)],
        compiler_params=_params(2),
        name="logf_prompt",
    )(h, w_f, b_f)


def _logf_sample_kernel(h_ref, w_ref, b_ref, lf_ref, cum_ref, *, rows, n_seq, n_heads):
    lf = _log_sigmoid(jnp.dot(h_ref[...], w_ref[...], preferred_element_type=F32) + b_ref[...])
    row = lax.broadcasted_iota(jnp.int32, (rows, rows), 0)
    col = lax.broadcasted_iota(jnp.int32, (rows, rows), 1)
    tri = jnp.logical_and((row % n_seq) == (col % n_seq), col <= row).astype(F32)
    cum = jnp.dot(tri, lf, preferred_element_type=F32, precision=lax.Precision.HIGHEST)
    lf_ref[...] = lf[:, :n_heads]
    cum_ref[...] = cum[:, :n_heads]


def _logf_sample(h, w_f, b_f, n_seq, n_heads):
    rows, d = h.shape
    return pl.pallas_call(
        functools.partial(_logf_sample_kernel, rows=rows, n_seq=n_seq, n_heads=n_heads),
        grid=(1,),
        in_specs=[pl.BlockSpec((rows, d), lambda i: (0, 0)),
                  pl.BlockSpec((d, LANES), lambda i: (0, 0)),
                  pl.BlockSpec((1, LANES), lambda i: (0, 0))],
        out_specs=[pl.BlockSpec((rows, n_heads), lambda i: (0, 0))] * 2,
        out_shape=[jax.ShapeDtypeStruct((rows, n_heads), F32)] * 2,
        compiler_params=_params(1),
        name="logf_sample",
    )(h, w_f, b_f)


def _page_suffix_kernel(x_ref, after_ref, total_ref, *, n_heads):
    x = x_ref[...]
    n_sub = x.shape[1]
    lane = lax.broadcasted_iota(jnp.int32, x.shape, 2)
    sub = lax.broadcasted_iota(jnp.int32, x.shape, 1)
    incl = x
    rows = x
    sh = n_heads
    while sh < LANES:
        incl = incl + jnp.where(lane + sh < LANES, pltpu.roll(incl, LANES - sh, axis=2), 0.0)
        rows = rows + pltpu.roll(rows, sh, axis=2)
        sh *= 2
    below = rows
    total = rows
    sh = 1
    while sh < n_sub:
        below = below + jnp.where(sub + sh < n_sub, pltpu.roll(below, n_sub - sh, axis=1), 0.0)
        total = total + pltpu.roll(total, sh, axis=1)
        sh *= 2
    after_ref[...] = (incl - x) + (below - rows)
    total_ref[...] = total


def _page_suffix(logf_pages, n_heads):
    n_phys, n_sub, _ = logf_pages.shape
    tp = 256
    blk = pl.BlockSpec((tp, n_sub, LANES), lambda i: (i, 0, 0))
    return pl.pallas_call(
        functools.partial(_page_suffix_kernel, n_heads=n_heads),
        grid=(n_phys // tp,),
        in_specs=[blk],
        out_specs=[blk, blk],
        out_shape=[jax.ShapeDtypeStruct(logf_pages.shape, F32)] * 2,
        compiler_params=_params(1),
        name="page_suffix",
    )(logf_pages)


def _pool_prompt_kernel(u_ref, halo_ref, wp_ref, ps_ref, o_ref, ext_ref, *, tm, group):
    t = pl.program_id(1)
    ext_ref[0:HALO, :] = jnp.where(t > 0, halo_ref[...], 0.0)
    ext_ref[HALO:HALO + tm, :] = u_ref[...]
    pos = t * tm + lax.broadcasted_iota(jnp.int32, (tm, 1), 0)
    for g, w in enumerate(POOL_WINDOWS):
        cs = slice(g * group, (g + 1) * group)
        cur = u_ref[:, cs]
        win = cur
        for j in range(1, w):
            win = win + ext_ref[HALO - j:HALO - j + tm, cs]
        cnt = jnp.minimum(pos + 1, w).astype(F32)
        d = win / cnt - cur
        y = jnp.dot(d.astype(BF16), wp_ref[g], preferred_element_type=F32) * ps_ref[:, cs]
        o_ref[:, cs] = y.astype(BF16)


def _pool_prompt(u, w_pool_bf, pool_scale, seq):
    m, width = u.shape
    n_groups, group, _ = w_pool_bf.shape
    tm = 256
    tiles = seq // tm
    return pl.pallas_call(
        functools.partial(_pool_prompt_kernel, tm=tm, group=group),
        grid=(m // seq, tiles),
        in_specs=[pl.BlockSpec((tm, width), lambda b, t: (b * tiles + t, 0)),
                  pl.BlockSpec((HALO, width),
                               lambda b, t: (jnp.maximum((b * tiles + t) * (tm // HALO) - 1, 0), 0)),
                  pl.BlockSpec((n_groups, group, group), lambda b, t: (0, 0, 0)),
                  pl.BlockSpec((1, width), lambda b, t: (0, 0))],
        out_specs=pl.BlockSpec((tm, width), lambda b, t: (b * tiles + t, 0)),
        out_shape=jax.ShapeDtypeStruct((m, width), BF16),
        scratch_shapes=[pltpu.VMEM((HALO + tm, width), F32)],
        compiler_params=_params(2),
        name="pool_prompt",
    )(u, u, w_pool_bf, pool_scale.reshape(1, width))


def _pool_sample_kernel(ext_ref, wp_ref, ps_ref, o_ref, *, t_new, n_seq):
    g = pl.program_id(0)
    for gi, w in enumerate(POOL_WINDOWS):
        @pl.when(g == gi)
        def _(w=w):
            ds = []
            for t in range(t_new):
                cur = ext_ref[POOL_HIST + t]
                win = cur
                for j in range(1, w):
                    win = win + ext_ref[POOL_HIST + t - j]
                ds.append(win / float(w) - cur)
            d = jnp.concatenate(ds, axis=0).astype(BF16)
            y = jnp.dot(d, wp_ref[...], preferred_element_type=F32) * ps_ref[...]
            for t in range(t_new):
                o_ref[t] = y[t * n_seq:(t + 1) * n_seq].astype(BF16)


def _pool_sample(ext_t, w_pool_bf, pool_scale, t_new):
    rows, n_seq, width = ext_t.shape
    n_groups, group, _ = w_pool_bf.shape
    return pl.pallas_call(
        functools.partial(_pool_sample_kernel, t_new=t_new, n_seq=n_seq),
        grid=(n_groups,),
        in_specs=[pl.BlockSpec((rows, n_seq, group), lambda g: (0, 0, g)),
                  pl.BlockSpec((None, group, group), lambda g: (g, 0, 0)),
                  pl.BlockSpec((1, group), lambda g: (0, g))],
        out_specs=pl.BlockSpec((t_new, n_seq, group), lambda g: (0, 0, g)),
        out_shape=jax.ShapeDtypeStruct((t_new, n_seq, width), BF16),
        compiler_params=_params(1),
        name="pool_sample",
    )(ext_t, w_pool_bf, pool_scale.reshape(1, width))


def _softmax_step(s, v, m_ref, l_ref, acc_ref):
    m_prev = m_ref[...]
    m_new = jnp.maximum(m_prev, jnp.max(s, axis=-1, keepdims=True))
    alpha = jnp.exp(m_prev - m_new)
    p = jnp.exp(s - m_new)
    l_ref[...] = alpha * l_ref[...] + jnp.sum(p, axis=-1, keepdims=True)
    acc_ref[...] = alpha * acc_ref[...] + jnp.dot(p.astype(BF16), v, preferred_element_type=F32)
    m_ref[...] = m_new


def _attn_prompt_kernel(q_ref, k_ref, v_ref, frow_ref, fcol_ref, o_ref, *, seq, tq, n_heads):
    h = pl.program_id(1)
    lane = lax.broadcasted_iota(jnp.int32, (seq, n_heads), 1)
    fcol_all = jnp.sum(jnp.where(lane == h, fcol_ref[...], 0.0), axis=1, keepdims=True) * LOG2E
    frow_all = frow_ref[pl.ds(h, 1), :] * LOG2E
    row = lax.broadcasted_iota(jnp.int32, (tq, tq), 0)
    col = lax.broadcasted_iota(jnp.int32, (tq, tq), 1)
    causal = col <= row

    def scores(q, fcol, lo, hi):
        s = lax.dot_general(q, k_ref[lo:hi, :], NT_DIMS, preferred_element_type=F32)
        return s * (ATTN_SCALE * LOG2E) + fcol - frow_all[:, lo:hi]

    for qi in range(seq // tq):
        lo, hi = qi * tq, (qi + 1) * tq
        q = q_ref[lo:hi, :]
        fcol = fcol_all[lo:hi]
        s_diag = jnp.where(causal, scores(q, fcol, lo, hi), NEG_INF)
        m = jnp.max(s_diag, axis=-1, keepdims=True)
        if qi > 0:
            s_past = scores(q, fcol, 0, lo)
            m = jnp.maximum(m, jnp.max(s_past, axis=-1, keepdims=True))
            p = jnp.exp2(s_past - m)
            l = jnp.sum(p, axis=-1, keepdims=True)
            acc = jnp.dot(p.astype(BF16), v_ref[0:lo, :], preferred_element_type=F32)
        p = jnp.exp2(s_diag - m)
        if qi > 0:
            l = l + jnp.sum(p, axis=-1, keepdims=True)
            acc = acc + jnp.dot(p.astype(BF16), v_ref[lo:hi, :], preferred_element_type=F32)
        else:
            l = jnp.sum(p, axis=-1, keepdims=True)
            acc = jnp.dot(p.astype(BF16), v_ref[lo:hi, :], preferred_element_type=F32)
        o_ref[lo:hi, :] = (acc / l).astype(BF16)


def _attn_prompt(q, k, v, cum_t, cum, seq, n_heads):
    m, width = q.shape
    tq = 512
    blk = pl.BlockSpec((seq, HEAD_DIM), lambda b, h: (b, h))
    return pl.pallas_call(
        functools.partial(_attn_prompt_kernel, seq=seq, tq=tq, n_heads=n_heads),
        grid=(m // seq, n_heads),
        in_specs=[blk, blk, blk,
                  pl.BlockSpec((None, n_heads, seq), lambda b, h: (b, 0, 0)),
                  pl.BlockSpec((seq, n_heads), lambda b, h: (b, 0))],
        out_specs=blk,
        out_shape=jax.ShapeDtypeStruct((m, width), BF16),
        compiler_params=_params(2),
        name="attn_prompt",
    )(q, k, v, cum_t, cum)


def _attn_sample_kernel(pt_ref, q_ref, *refs, n_pp, t_new, n_heads):
    k_refs = refs[:n_pp]
    v_refs = refs[n_pp:2 * n_pp]
    after_refs = refs[2 * n_pp:3 * n_pp]
    total_refs = refs[3 * n_pp:4 * n_pp]
    kn_ref, vn_ref, cnrow_ref, cncol_ref, o_ref = refs[4 * n_pp:4 * n_pp + 5]
    mb_ref, m_ref, l_ref, acc_ref, knp_ref, vnp_ref, later_ref = refs[4 * n_pp + 5:]
    c = pl.program_id(1)
    n_rows = n_heads * t_new
    n_cols = PAGE_SIZE * n_heads

    @pl.when(c == 0)
    def _():
        row = lax.broadcasted_iota(jnp.int32, (n_rows, n_cols), 0)
        col = lax.broadcasted_iota(jnp.int32, (n_rows, n_cols), 1)
        same_head = (row // t_new) == (col % n_heads)
        mb_ref[...] = jnp.where(same_head, cncol_ref[...], NEG_INF)
        m_ref[...] = jnp.full_like(m_ref, NEG_INF)
        l_ref[...] = jnp.zeros_like(l_ref)
        acc_ref[...] = jnp.zeros_like(acc_ref)
        later_ref[...] = jnp.zeros_like(later_ref)

    q = q_ref[...]
    s_parts = []
    for i in range(n_pp):
        kp = k_refs[i][...].astype(BF16)
        s = lax.dot_general(q, kp, NT_DIMS, preferred_element_type=F32)
        bias = after_refs[i][...] + later_ref[...]
        later_ref[...] += total_refs[i][...]
        chunks = []
        for j in range(n_cols // LANES):
            cs = slice(j * LANES, (j + 1) * LANES)
            chunks.append(s[:, cs] * ATTN_SCALE + mb_ref[:, cs] + bias[j:j + 1, :])
        s_parts.append(jnp.concatenate(chunks, axis=1))
    m_prev = m_ref[...]
    m_new = m_prev
    for s in s_parts:
        m_new = jnp.maximum(m_new, jnp.max(s, axis=-1, keepdims=True))
    alpha = jnp.exp(m_prev - m_new)
    l_new = alpha * l_ref[...]
    acc = alpha * acc_ref[...]
    for i, s in enumerate(s_parts):
        p = jnp.exp(s - m_new)
        l_new = l_new + jnp.sum(p, axis=-1, keepdims=True)
        acc = acc + jnp.dot(p.astype(BF16), v_refs[i][...].astype(BF16),
                            preferred_element_type=F32)
    m_ref[...] = m_new
    l_ref[...] = l_new
    acc_ref[...] = acc

    @pl.when(c == pl.num_programs(1) - 1)
    def _():
        n_new = t_new * n_heads
        knp_ref[0:n_new, :] = kn_ref[...].astype(BF16)
        knp_ref[n_new:, :] = jnp.zeros((LANES - n_new, HEAD_DIM), BF16)
        vnp_ref[0:n_new, :] = vn_ref[...].astype(BF16)
        vnp_ref[n_new:, :] = jnp.zeros((LANES - n_new, HEAD_DIM), BF16)
        s = lax.dot_general(q, knp_ref[...], NT_DIMS, preferred_element_type=F32)
        row = lax.broadcasted_iota(jnp.int32, (n_rows, LANES), 0)
        col = lax.broadcasted_iota(jnp.int32, (n_rows, LANES), 1)
        valid = jnp.logical_and((row // t_new) == (col % n_heads),
                                (col // n_heads) <= (row % t_new))
        s = jnp.where(valid, s * ATTN_SCALE + cncol_ref[...] - cnrow_ref[...], NEG_INF)
        _softmax_step(s, vnp_ref[...], m_ref, l_ref, acc_ref)
        o_ref[...] = (acc_ref[...] / l_ref[...]).astype(BF16)


def _attn_sample(page_table, q_rows, k_pages, v_pages, lf_after, lf_total, k_new, v_new,
                 cn_row, cn_col, t_new, n_heads):
    n_seq, n_pages = page_table.shape
    n_rows = n_heads * t_new
    n_cols = PAGE_SIZE * n_heads
    n_sub = n_cols // LANES
    n_new = t_new * n_heads
    n_pp = 4
    steps = n_pages // n_pp

    def page_idx(b, c, pt, i):
        return pt[b, n_pages - 1 - (c * n_pp + i)]

    def page_spec(i):
        return pl.BlockSpec((None, n_cols, HEAD_DIM), lambda b, c, pt: (page_idx(b, c, pt, i), 0, 0))

    def lf_spec(i):
        return pl.BlockSpec((None, n_sub, LANES), lambda b, c, pt: (page_idx(b, c, pt, i), 0, 0))

    def seq_spec(r, w):
        return pl.BlockSpec((None, r, w), lambda b, c, pt: (b, 0, 0))

    return pl.pallas_call(
        functools.partial(_attn_sample_kernel, n_pp=n_pp, t_new=t_new, n_heads=n_heads),
        grid_spec=pltpu.PrefetchScalarGridSpec(
            num_scalar_prefetch=1,
            grid=(n_seq, steps),
            in_specs=([seq_spec(n_rows, HEAD_DIM)]
                      + [page_spec(i) for i in range(n_pp)] * 2
                      + [lf_spec(i) for i in range(n_pp)] * 2
                      + [seq_spec(n_new, HEAD_DIM), seq_spec(n_new, HEAD_DIM),
                         seq_spec(1, LANES), seq_spec(n_rows, 1)]),
            out_specs=seq_spec(n_rows, HEAD_DIM),
            scratch_shapes=[pltpu.VMEM((n_rows, n_cols), F32),
                            pltpu.VMEM((n_rows, 1), F32), pltpu.VMEM((n_rows, 1), F32),
                            pltpu.VMEM((n_rows, HEAD_DIM), F32),
                            pltpu.VMEM((LANES, HEAD_DIM), BF16),
                            pltpu.VMEM((LANES, HEAD_DIM), BF16),
                            pltpu.VMEM((n_sub, LANES), F32)],
        ),
        out_shape=jax.ShapeDtypeStruct((n_seq, n_rows, HEAD_DIM), BF16),
        compiler_params=_params(2),
        name="attn_sample",
    )(page_table, q_rows, *([k_pages] * n_pp), *([v_pages] * n_pp),
      *([lf_after] * n_pp), *([lf_total] * n_pp), k_new, v_new, cn_row, cn_col)


def _outproj_kernel(a_ref, p_ref, wa_ref, wp_ref, x_ref, ga_ref, o_ref, wabf_ref, wpbf_ref):
    @pl.when(pl.program_id(1) == 0)
    def _():
        wabf_ref[...] = wa_ref[...].astype(BF16)
        wpbf_ref[...] = wp_ref[...].astype(BF16)

    y = jnp.dot(a_ref[...], wabf_ref[...], preferred_element_type=F32)
    y = y + jnp.dot(p_ref[...], wpbf_ref[...], preferred_element_type=F32)
    for rows in _row_groups(x_ref.shape[0], ga_ref):
        o_ref[rows, :] = x_ref[rows, :] + ga_ref[...] * y[rows, :]


def _outproj(a, pm, w_out, x, mod, rows_per_batch, n_seq, ga_chunk):
    m, ka = a.shape
    kp = pm.shape[1]
    d = w_out.shape[1]
    tm, tn = 512, 512
    tiles = rows_per_batch // tm
    ga_col0 = ga_chunk * (d // tn)
    return pl.pallas_call(
        _outproj_kernel,
        grid=(d // tn, m // tm),
        in_specs=[pl.BlockSpec((tm, ka), lambda j, i: (i, 0)),
                  pl.BlockSpec((tm, kp), lambda j, i: (i, 0)),
                  pl.BlockSpec((ka, tn), lambda j, i: (0, j)),
                  pl.BlockSpec((kp, tn), lambda j, i: (ka // kp, j)),
                  pl.BlockSpec((tm, tn), lambda j, i: (i, j)),
                  pl.BlockSpec(_mod_block(mod, tn, n_seq),
                               lambda j, i: _mod_index(mod, i // tiles, ga_col0 + j))],
        out_specs=pl.BlockSpec((tm, tn), lambda j, i: (i, j)),
        out_shape=jax.ShapeDtypeStruct((m, d), F32),
        scratch_shapes=[pltpu.VMEM((ka, tn), BF16), pltpu.VMEM((kp, tn), BF16)],
        compiler_params=_params(2),
        name="out_proj",
    )(a, pm, w_out, w_out, x, mod)


def _ffn_kernel(h_ref, wu_ref, wd_ref, x_ref, ga_ref, o_ref, hid_ref, *, n_up, tf):
    s = pl.program_id(1)

    @pl.when(s < n_up)
    def _():
        a = jnp.dot(h_ref[...], wu_ref[...], preferred_element_type=F32)
        col = pl.multiple_of(s * tf, tf)
        hid_ref[:, pl.ds(col, tf)] = jnp.square(jnp.maximum(a, 0.0)).astype(BF16)

    @pl.when(s >= n_up)
    def _():
        y = jnp.dot(hid_ref[...], wd_ref[...], preferred_element_type=F32)
        for rows in _row_groups(x_ref.shape[0], ga_ref):
            o_ref[rows, :] = x_ref[rows, :] + ga_ref[...] * y[rows, :]


def _ffn(h, x, w_up_bf, w_down_bf, mod, rows_per_batch, n_seq, ga_chunk):
    m, d = x.shape
    d_ff = w_up_bf.shape[1]
    tm, tf, tn = 512, 512, 256
    tiles = rows_per_batch // tm
    n_up, n_dn = d_ff // tf, d // tn
    ga_col0 = ga_chunk * n_dn

    def up_step(s):
        return jnp.minimum(s, n_up - 1)

    def dn_step(s):
        return jnp.maximum(s - n_up, 0)

    return pl.pallas_call(
        functools.partial(_ffn_kernel, n_up=n_up, tf=tf),
        grid=(m // tm, n_up + n_dn),
        in_specs=[pl.BlockSpec((tm, d), lambda i, s: (i, 0)),
                  pl.BlockSpec((d, tf), lambda i, s: (0, up_step(s))),
                  pl.BlockSpec((d_ff, tn), lambda i, s: (0, dn_step(s))),
                  pl.BlockSpec((tm, tn), lambda i, s: (i, dn_step(s))),
                  pl.BlockSpec(_mod_block(mod, tn, n_seq),
                               lambda i, s: _mod_index(mod, i // tiles, ga_col0 + dn_step(s)))],
        out_specs=pl.BlockSpec((tm, tn), lambda i, s: (i, dn_step(s))),
        out_shape=jax.ShapeDtypeStruct((m, d), F32),
        scratch_shapes=[pltpu.VMEM((tm, d_ff), BF16)],
        compiler_params=_params(2),
        name="ffn",
    )(h, w_up_bf, w_down_bf, x, mod)


def _pad_cols(w, n):
    return jnp.pad(w, ((0, 0), (0, n - w.shape[1])))


def kernel(x_prompt, x_sample, c_prompt, c_sample, cache_k, cache_v, cache_logf, state_pool,
           page_table, w_ada, b_ada, g_mix, w_in, b_f, w_pool, pool_scale, w_out, g_ffn,
           w_up, w_down, g_final):
    n_b, seq, d = x_prompt.shape
    n_s, t_new, _ = x_sample.shape
    assert w_ada.shape[0] == 1, "single trunk layer"
    l = 0
    n_heads = b_f.shape[1]
    attn_w = n_heads * HEAD_DIM
    pool_w = pool_scale.shape[1]
    n_phys = cache_k.shape[1]
    n_pages = page_table.shape[1]
    n_cols = PAGE_SIZE * n_heads
    SH1, SC1, GA1, SH2, SC2, GA2 = range(N_ADA)

    w_in_l = w_in[l]
    w_f = _pad_cols(w_in_l[:, 3 * attn_w:3 * attn_w + n_heads], LANES).astype(BF16)
    b_f_p = _pad_cols(b_f[l].reshape(1, n_heads), LANES)
    w_u = w_in_l[:, 3 * attn_w + n_heads:]
    w_pool_bf = w_pool[l].astype(BF16)
    w_up_bf = w_up[l].astype(BF16)
    w_down_bf = w_down[l].astype(BF16)

    n_c = n_s + n_b
    c_rows = -(-n_c // 16) * 16
    c_all = jnp.pad(jnp.concatenate([c_sample, c_prompt], axis=0), ((0, c_rows - n_c), (0, 0)))
    mod_s = _modulation(c_all, w_ada[l], b_ada[l])
    mod_p = mod_s[n_s:n_c].reshape(n_b, 1, N_ADA * d)

    def trunk_front(x2, mod_x, rows_per_batch):
        h = _norm_mod(x2, g_mix[l], mod_x, rows_per_batch, n_s, SC1, SH1)
        (q,) = _proj(h, w_in_l, 0, attn_w, [BF16], "proj_q")
        k, k_bf = _proj(h, w_in_l, attn_w, attn_w, [F32, BF16], "proj_k")
        v, v_bf = _proj(h, w_in_l, 2 * attn_w, attn_w, [F32, BF16], "proj_v")
        (u,) = _proj(h, w_u, 0, pool_w, [F32], "proj_u")
        return h, q, k, k_bf, v, v_bf, u

    def trunk_back(a, pm, x2, mod_x, rows_per_batch):
        x1 = _outproj(a, pm, w_out[l], x2, mod_x, rows_per_batch, n_s, GA1)
        h2 = _norm_mod(x1, g_ffn[l], mod_x, rows_per_batch, n_s, SC2, SH2)
        x3 = _ffn(h2, x1, w_up_bf, w_down_bf, mod_x, rows_per_batch, n_s, GA2)
        return _final_norm(x3, g_final)

    xp = x_prompt.reshape(n_b * seq, d)
    h, q, k_p, k_bf, v_p, v_bf, u_p = trunk_front(xp, mod_p, seq)
    logf_p, cum_p = _logf_prompt(h, w_f, b_f_p, seq, n_heads)
    cum_t = cum_p.reshape(n_b, seq, n_heads).transpose(0, 2, 1)
    a_p = _attn_prompt(q, k_bf, v_bf, cum_t, cum_p, seq, n_heads)
    pm_p = _pool_prompt(u_p, w_pool_bf, pool_scale[l], seq)
    y_p = trunk_back(a_p, pm_p, xp, mod_p, seq)
    hist_p = u_p.reshape(n_b, seq, pool_w)[:, seq - POOL_HIST:]

    def seq_major(x2):
        return x2.reshape(t_new, n_s, x2.shape[-1]).transpose(1, 0, 2)

    xs = x_sample.transpose(1, 0, 2).reshape(t_new * n_s, d)
    h, q, k_s, _, v_s, _, u_s = trunk_front(xs, mod_s, t_new * n_s)
    logf_s, cn = _logf_sample(h, w_f, b_f_p, n_s, n_heads)
    k_s, v_s, logf_s, cn = seq_major(k_s), seq_major(v_s), seq_major(logf_s), seq_major(cn)

    lf_after, lf_total = _page_suffix(
        cache_logf[l].reshape(n_phys, n_cols // LANES, LANES), n_heads)
    cn_row = _pad_cols(cn.reshape(n_s, t_new * n_heads), LANES).reshape(n_s, 1, LANES)
    cn_col = cn.transpose(0, 2, 1).reshape(n_s, n_heads * t_new, 1)
    q_rows = q.reshape(t_new, n_s, n_heads, HEAD_DIM).transpose(1, 2, 0, 3)
    q_rows = q_rows.reshape(n_s, n_heads * t_new, HEAD_DIM)
    a_s = _attn_sample(page_table, q_rows,
                       cache_k[l].reshape(n_phys, n_cols, HEAD_DIM),
                       cache_v[l].reshape(n_phys, n_cols, HEAD_DIM),
                       lf_after, lf_total,
                       k_s.reshape(n_s, t_new * n_heads, HEAD_DIM),
                       v_s.reshape(n_s, t_new * n_heads, HEAD_DIM),
                       cn_row, cn_col, t_new, n_heads)
    a_s = a_s.reshape(n_s, n_heads, t_new, HEAD_DIM).transpose(2, 0, 1, 3)
    a_s = a_s.reshape(t_new * n_s, attn_w)

    ext_t = jnp.concatenate([state_pool[l].transpose(1, 0, 2),
                             u_s.reshape(t_new, n_s, pool_w)], axis=0)
    pm_s = _pool_sample(ext_t, w_pool_bf, pool_scale[l], t_new).reshape(t_new * n_s, pool_w)
    y_s = trunk_back(a_s, pm_s, xs, mod_s, t_new * n_s)
    hist_s = ext_t[t_new:].transpose(1, 0, 2)

    kv_p = (1, n_b, seq, n_heads, HEAD_DIM)
    kv_s = (1, n_s, t_new, n_heads, HEAD_DIM)
    return (y_p.reshape(n_b, seq, d), seq_major(y_s),
            k_p.reshape(kv_p), v_p.reshape(kv_p), logf_p.reshape(1, n_b, seq, n_heads), hist_p[None],
            k_s.reshape(kv_s), v_s.reshape(kv_s), logf_s[None], hist_s[None])
```

```python
import functools

import jax
import jax.numpy as jnp
from jax import lax
from jax.experimental import pallas as pl
from jax.experimental.pallas import tpu as pltpu

F32 = jnp.float32
BF16 = jnp.bfloat16

HEAD_DIM = 128
POOL_WINDOWS = (2, 4, 8, 16)
POOL_HIST = max(POOL_WINDOWS) - 1
HALO = 16
RMS_EPS = 1e-6
N_ADA = 6
PAGE_SIZE = 128
ATTN_SCALE = HEAD_DIM ** -0.5
LANES = 128
V7X_VMEM_LIMIT = 56 * 1024 * 1024
NEG_INF = float("-inf")
LOG2E = 1.4426950408889634
NT_DIMS = (((1,), (1,)), ((), ()))


def _params(n_grid):
    return pltpu.CompilerParams(
        dimension_semantics=("arbitrary",) * n_grid, vmem_limit_bytes=V7X_VMEM_LIMIT)


def _log_sigmoid(z):
    return jnp.minimum(z, 0.0) - jnp.log1p(jnp.exp(-jnp.abs(z)))


def _rms_scale(x):
    return lax.rsqrt(jnp.mean(x * x, axis=-1, keepdims=True) + RMS_EPS)


def _mod_block(mod, tn, n_seq):
    return (n_seq, tn) if mod.ndim == 2 else (None, 1, tn)


def _mod_index(mod, batch_idx, col_blk):
    return (0, col_blk) if mod.ndim == 2 else (batch_idx, 0, col_blk)


def _row_groups(n_rows, mod_ref):
    group = n_rows if mod_ref.shape[0] == 1 else mod_ref.shape[0]
    return [slice(r, r + group) for r in range(0, n_rows, group)]


def _mod_kernel(c_ref, w_ref, b_ref, o_ref, s_ref):
    @pl.when(pl.program_id(0) == 0)
    def _():
        c = c_ref[...]
        s_ref[...] = (c * jax.nn.sigmoid(c)).astype(BF16)

    o_ref[...] = jnp.dot(s_ref[...], w_ref[...].astype(BF16),
                         preferred_element_type=F32) + b_ref[...]


def _modulation(c_all, w_ada, b_ada):
    rows, d = c_all.shape
    n = w_ada.shape[1]
    tn = 512
    return pl.pallas_call(
        _mod_kernel,
        grid=(n // tn,),
        in_specs=[pl.BlockSpec((rows, d), lambda j: (0, 0)),
                  pl.BlockSpec((d, tn), lambda j: (0, j)),
                  pl.BlockSpec((1, tn), lambda j: (0, j))],
        out_specs=pl.BlockSpec((rows, tn), lambda j: (0, j)),
        out_shape=jax.ShapeDtypeStruct((rows, n), F32),
        scratch_shapes=[pltpu.VMEM((rows, d), BF16)],
        compiler_params=_params(1),
        name="adaln_mod",
    )(c_all, w_ada, b_ada.reshape(1, n))


def _norm_kernel(x_ref, g_ref, sc_ref, sh_ref, o_ref):
    for rows in _row_groups(x_ref.shape[0], sc_ref):
        x = x_ref[rows, :]
        h = x * _rms_scale(x) * g_ref[...]
        o_ref[rows, :] = (h * (1.0 + sc_ref[...]) + sh_ref[...]).astype(BF16)


def _norm_mod(x, g, mod, rows_per_batch, n_seq, sc_chunk, sh_chunk):
    m, d = x.shape
    tm = 256
    tiles = rows_per_batch // tm

    def mod_spec(chunk):
        return pl.BlockSpec(_mod_block(mod, d, n_seq),
                            lambda i: _mod_index(mod, i // tiles, chunk))

    return pl.pallas_call(
        _norm_kernel,
        grid=(m // tm,),
        in_specs=[pl.BlockSpec((tm, d), lambda i: (i, 0)),
                  pl.BlockSpec((1, d), lambda i: (0, 0)),
                  mod_spec(sc_chunk), mod_spec(sh_chunk)],
        out_specs=pl.BlockSpec((tm, d), lambda i: (i, 0)),
        out_shape=jax.ShapeDtypeStruct((m, d), BF16),
        compiler_params=_params(1),
        name="norm_mod",
    )(x, g.reshape(1, d), mod, mod)


def _final_norm_kernel(x_ref, g_ref, o_ref):
    x = x_ref[...]
    o_ref[...] = x * _rms_scale(x) * g_ref[...]


def _final_norm(x, g):
    m, d = x.shape
    tm = 256
    return pl.pallas_call(
        _final_norm_kernel,
        grid=(m // tm,),
        in_specs=[pl.BlockSpec((tm, d), lambda i: (i, 0)),
                  pl.BlockSpec((1, d), lambda i: (0, 0))],
        out_specs=pl.BlockSpec((tm, d), lambda i: (i, 0)),
        out_shape=jax.ShapeDtypeStruct((m, d), F32),
        compiler_params=_params(1),
        name="final_norm",
    )(x, g.reshape(1, d))


def _proj_kernel(x_ref, w_ref, *rest, shift):
    if shift:
        wnext_ref, rest = rest[0], rest[1:]
    out_refs, wbf_ref = rest[:-1], rest[-1]
    tn = wbf_ref.shape[0]

    @pl.when(pl.program_id(1) == 0)
    def _():
        wbf_ref[0:tn - shift, :] = w_ref[shift:tn, :].astype(BF16)
        if shift:
            wbf_ref[tn - shift:tn, :] = wnext_ref[0:shift, :].astype(BF16)

    acc = lax.dot_general(x_ref[...], wbf_ref[...], NT_DIMS, preferred_element_type=F32)
    for o_ref in out_refs:
        o_ref[...] = acc.astype(o_ref.dtype)


def _proj(x, w_t, row0, n, out_dtypes, name):
    m, k = x.shape
    tm, tn = min(m, 1024), 512
    off, shift = row0 // tn, row0 % tn
    assert shift % 16 == 0 and shift <= LANES
    in_specs = [pl.BlockSpec((tm, k), lambda j, i: (i, 0)),
                pl.BlockSpec((tn, k), lambda j, i: (off + j, 0))]
    operands = [x, w_t]
    if shift:
        in_specs.append(pl.BlockSpec((LANES, k), lambda j, i: ((off + j + 1) * (tn // LANES), 0)))
        operands.append(w_t)
    return pl.pallas_call(
        functools.partial(_proj_kernel, shift=shift),
        grid=(n // tn, m // tm),
        in_specs=in_specs,
        out_specs=[pl.BlockSpec((tm, tn), lambda j, i: (i, j)) for _ in out_dtypes],
        out_shape=[jax.ShapeDtypeStruct((m, n), dt) for dt in out_dtypes],
        scratch_shapes=[pltpu.VMEM((tn, k), BF16)],
        compiler_params=_params(2),
        name=name,
    )(*operands)


def _logf_prompt_kernel(h_ref, w_ref, b_ref, lf_ref, cum_ref, carry_ref, *, tm, n_heads):
    @pl.when(pl.program_id(1) == 0)
    def _():
        carry_ref[...] = jnp.zeros_like(carry_ref)

    lf = _log_sigmoid(lax.dot_general(h_ref[...], w_ref[...].astype(BF16), NT_DIMS,
                                      preferred_element_type=F32) + b_ref[...])
    row = lax.broadcasted_iota(jnp.int32, (tm, tm), 0)
    col = lax.broadcasted_iota(jnp.int32, (tm, tm), 1)
    tri = (col <= row).astype(F32)
    cum = jnp.dot(tri, lf, preferred_element_type=F32,
                  precision=lax.Precision.HIGHEST) + carry_ref[...]
    carry_ref[...] = cum[tm - 1:tm, :]
    lf_ref[...] = lf[:, :n_heads]
    cum_ref[...] = cum[:, :n_heads]


def _logf_prompt(h, w, gate_col, b_f, seq, n_heads):
    m, d = h.shape
    tm = 512
    tiles = seq // tm
    assert gate_col % LANES == 0
    return pl.pallas_call(
        functools.partial(_logf_prompt_kernel, tm=tm, n_heads=n_heads),
        grid=(m // seq, tiles),
        in_specs=[pl.BlockSpec((tm, d), lambda b, t: (b * tiles + t, 0)),
                  pl.BlockSpec((LANES, d), lambda b, t: (gate_col // LANES, 0)),
                  pl.BlockSpec((1, LANES), lambda b, t: (0, 0))],
        out_specs=[pl.BlockSpec((tm, n_heads), lambda b, t: (b * tiles + t, 0))] * 2,
        out_shape=[jax.ShapeDtypeStruct((m, n_heads), F32)] * 2,
        scratch_shapes=[pltpu.VMEM((1, LANES), F32)],
        compiler_params=_params(2),
        name="logf_prompt",
    )(h, w, b_f)


def _logf_sample_kernel(h_ref, w_ref, b_ref, lf_ref, cum_ref, *, rows, n_seq, n_heads):
    lf = _log_sigmoid(lax.dot_general(h_ref[...], w_ref[...].astype(BF16), NT_DIMS,
                                      preferred_element_type=F32) + b_ref[...])
    row = lax.broadcasted_iota(jnp.int32, (rows, rows), 0)
    col = lax.broadcasted_iota(jnp.int32, (rows, rows), 1)
    tri = jnp.logical_and((row % n_seq) == (col % n_seq), col <= row).astype(F32)
    cum = jnp.dot(tri, lf, preferred_element_type=F32, precision=lax.Precision.HIGHEST)
    lf_ref[...] = lf[:, :n_heads]
    cum_ref[...] = cum[:, :n_heads]


def _logf_sample(h, w, gate_col, b_f, n_seq, n_heads):
    rows, d = h.shape
    assert gate_col % LANES == 0
    return pl.pallas_call(
        functools.partial(_logf_sample_kernel, rows=rows, n_seq=n_seq, n_heads=n_heads),
        grid=(1,),
        in_specs=[pl.BlockSpec((rows, d), lambda i: (0, 0)),
                  pl.BlockSpec((LANES, d), lambda i: (gate_col // LANES, 0)),
                  pl.BlockSpec((1, LANES), lambda i: (0, 0))],
        out_specs=[pl.BlockSpec((rows, n_heads), lambda i: (0, 0))] * 2,
        out_shape=[jax.ShapeDtypeStruct((rows, n_heads), F32)] * 2,
        compiler_params=_params(1),
        name="logf_sample",
    )(h, w, b_f)


def _page_suffix_kernel(x_ref, after_ref, total_ref, *, n_heads):
    x = x_ref[...]
    n_sub = x.shape[1]
    lane = lax.broadcasted_iota(jnp.int32, x.shape, 2)
    sub = lax.broadcasted_iota(jnp.int32, x.shape, 1)
    incl = x
    rows = x
    sh = n_heads
    while sh < LANES:
        incl = incl + jnp.where(lane + sh < LANES, pltpu.roll(incl, LANES - sh, axis=2), 0.0)
        rows = rows + pltpu.roll(rows, sh, axis=2)
        sh *= 2
    below = rows
    total = rows
    sh = 1
    while sh < n_sub:
        below = below + jnp.where(sub + sh < n_sub, pltpu.roll(below, n_sub - sh, axis=1), 0.0)
        total = total + pltpu.roll(total, sh, axis=1)
        sh *= 2
    after_ref[...] = (incl - x) + (below - rows)
    total_ref[...] = total


def _page_suffix(logf_pages, n_heads):
    n_phys, n_sub, _ = logf_pages.shape
    tp = 256
    blk = pl.BlockSpec((tp, n_sub, LANES), lambda i: (i, 0, 0))
    return pl.pallas_call(
        functools.partial(_page_suffix_kernel, n_heads=n_heads),
        grid=(n_phys // tp,),
        in_specs=[blk],
        out_specs=[blk, blk],
        out_shape=[jax.ShapeDtypeStruct(logf_pages.shape, F32)] * 2,
        compiler_params=_params(1),
        name="page_suffix",
    )(logf_pages)


def _pool_prompt_kernel(u_ref, halo_ref, wp_ref, ps_ref, o_ref, ext_ref, *, tm, group):
    t = pl.program_id(1)
    ext_ref[0:HALO, :] = jnp.where(t > 0, halo_ref[...], 0.0)
    ext_ref[HALO:HALO + tm, :] = u_ref[...]
    pos = t * tm + lax.broadcasted_iota(jnp.int32, (tm, 1), 0)
    for g, w in enumerate(POOL_WINDOWS):
        cs = slice(g * group, (g + 1) * group)
        cur = u_ref[:, cs]
        win = cur
        for j in range(1, w):
            win = win + ext_ref[HALO - j:HALO - j + tm, cs]
        cnt = jnp.minimum(pos + 1, w).astype(F32)
        d = win / cnt - cur
        y = jnp.dot(d.astype(BF16), wp_ref[g], preferred_element_type=F32) * ps_ref[:, cs]
        o_ref[:, cs] = y.astype(BF16)


def _pool_prompt(u, w_pool_bf, pool_scale, seq):
    m, width = u.shape
    n_groups, group, _ = w_pool_bf.shape
    tm = 256
    tiles = seq // tm
    return pl.pallas_call(
        functools.partial(_pool_prompt_kernel, tm=tm, group=group),
        grid=(m // seq, tiles),
        in_specs=[pl.BlockSpec((tm, width), lambda b, t: (b * tiles + t, 0)),
                  pl.BlockSpec((HALO, width),
                               lambda b, t: (jnp.maximum((b * tiles + t) * (tm // HALO) - 1, 0), 0)),
                  pl.BlockSpec((n_groups, group, group), lambda b, t: (0, 0, 0)),
                  pl.BlockSpec((1, width), lambda b, t: (0, 0))],
        out_specs=pl.BlockSpec((tm, width), lambda b, t: (b * tiles + t, 0)),
        out_shape=jax.ShapeDtypeStruct((m, width), BF16),
        scratch_shapes=[pltpu.VMEM((HALO + tm, width), F32)],
        compiler_params=_params(2),
        name="pool_prompt",
    )(u, u, w_pool_bf, pool_scale.reshape(1, width))


def _pool_sample_kernel(ext_ref, wp_ref, ps_ref, o_ref, *, t_new, n_seq):
    g = pl.program_id(0)
    for gi, w in enumerate(POOL_WINDOWS):
        @pl.when(g == gi)
        def _(w=w):
            ds = []
            for t in range(t_new):
                cur = ext_ref[POOL_HIST + t]
                win = cur
                for j in range(1, w):
                    win = win + ext_ref[POOL_HIST + t - j]
                ds.append(win / float(w) - cur)
            d = jnp.concatenate(ds, axis=0).astype(BF16)
            y = jnp.dot(d, wp_ref[...], preferred_element_type=F32) * ps_ref[...]
            for t in range(t_new):
                o_ref[t] = y[t * n_seq:(t + 1) * n_seq].astype(BF16)


def _pool_sample(ext_t, w_pool_bf, pool_scale, t_new):
    rows, n_seq, width = ext_t.shape
    n_groups, group, _ = w_pool_bf.shape
    return pl.pallas_call(
        functools.partial(_pool_sample_kernel, t_new=t_new, n_seq=n_seq),
        grid=(n_groups,),
        in_specs=[pl.BlockSpec((rows, n_seq, group), lambda g: (0, 0, g)),
                  pl.BlockSpec((None, group, group), lambda g: (g, 0, 0)),
                  pl.BlockSpec((1, group), lambda g: (0, g))],
        out_specs=pl.BlockSpec((t_new, n_seq, group), lambda g: (0, 0, g)),
        out_shape=jax.ShapeDtypeStruct((t_new, n_seq, width), BF16),
        compiler_params=_params(1),
        name="pool_sample",
    )(ext_t, w_pool_bf, pool_scale.reshape(1, width))


def _softmax_step(s, v, m_ref, l_ref, acc_ref):
    m_prev = m_ref[...]
    m_new = jnp.maximum(m_prev, jnp.max(s, axis=-1, keepdims=True))
    alpha = jnp.exp(m_prev - m_new)
    p = jnp.exp(s - m_new)
    l_ref[...] = alpha * l_ref[...] + jnp.sum(p, axis=-1, keepdims=True)
    acc_ref[...] = alpha * acc_ref[...] + jnp.dot(p.astype(BF16), v, preferred_element_type=F32)
    m_ref[...] = m_new


def _attn_prompt_kernel(q_ref, kf_ref, vf_ref, frow_ref, fcol_ref, o_ref, k_ref, v_ref,
                        *, seq, tq, n_heads):
    h = pl.program_id(1)
    k_ref[...] = kf_ref[...].astype(BF16)
    v_ref[...] = vf_ref[...].astype(BF16)
    lane = lax.broadcasted_iota(jnp.int32, (seq, n_heads), 1)
    fcol_all = jnp.sum(jnp.where(lane == h, fcol_ref[...], 0.0), axis=1, keepdims=True) * LOG2E
    frow_all = frow_ref[pl.ds(h, 1), :] * LOG2E
    row = lax.broadcasted_iota(jnp.int32, (tq, tq), 0)
    col = lax.broadcasted_iota(jnp.int32, (tq, tq), 1)
    causal = col <= row

    def scores(q, fcol, lo, hi):
        s = lax.dot_general(q, k_ref[lo:hi, :], NT_DIMS, preferred_element_type=F32)
        return s * (ATTN_SCALE * LOG2E) + fcol - frow_all[:, lo:hi]

    for qi in range(seq // tq):
        lo, hi = qi * tq, (qi + 1) * tq
        q = q_ref[lo:hi, :]
        fcol = fcol_all[lo:hi]
        s_diag = jnp.where(causal, scores(q, fcol, lo, hi), NEG_INF)
        m = jnp.max(s_diag, axis=-1, keepdims=True)
        if qi > 0:
            s_past = scores(q, fcol, 0, lo)
            m = jnp.maximum(m, jnp.max(s_past, axis=-1, keepdims=True))
            p = jnp.exp2(s_past - m)
            l = jnp.sum(p, axis=-1, keepdims=True)
            acc = jnp.dot(p.astype(BF16), v_ref[0:lo, :], preferred_element_type=F32)
        p = jnp.exp2(s_diag - m)
        if qi > 0:
            l = l + jnp.sum(p, axis=-1, keepdims=True)
            acc = acc + jnp.dot(p.astype(BF16), v_ref[lo:hi, :], preferred_element_type=F32)
        else:
            l = jnp.sum(p, axis=-1, keepdims=True)
            acc = jnp.dot(p.astype(BF16), v_ref[lo:hi, :], preferred_element_type=F32)
        o_ref[lo:hi, :] = (acc / l).astype(BF16)


def _attn_prompt(q, k, v, cum_t, cum, seq, n_heads):
    m, width = q.shape
    tq = 512
    blk = pl.BlockSpec((seq, HEAD_DIM), lambda b, h: (b, h))
    return pl.pallas_call(
        functools.partial(_attn_prompt_kernel, seq=seq, tq=tq, n_heads=n_heads),
        grid=(m // seq, n_heads),
        in_specs=[blk, blk, blk,
                  pl.BlockSpec((None, n_heads, seq), lambda b, h: (b, 0, 0)),
                  pl.BlockSpec((seq, n_heads), lambda b, h: (b, 0))],
        out_specs=blk,
        out_shape=jax.ShapeDtypeStruct((m, width), BF16),
        scratch_shapes=[pltpu.VMEM((seq, HEAD_DIM), BF16), pltpu.VMEM((seq, HEAD_DIM), BF16)],
        compiler_params=_params(2),
        name="attn_prompt",
    )(q, k, v, cum_t, cum)


def _attn_sample_kernel(pt_ref, q_ref, *refs, n_pp, t_new, n_heads):
    k_refs = refs[:n_pp]
    v_refs = refs[n_pp:2 * n_pp]
    after_refs = refs[2 * n_pp:3 * n_pp]
    total_refs = refs[3 * n_pp:4 * n_pp]
    kn_ref, vn_ref, cnrow_ref, cncol_ref, o_ref = refs[4 * n_pp:4 * n_pp + 5]
    mb_ref, m_ref, l_ref, acc_ref, knp_ref, vnp_ref, later_ref = refs[4 * n_pp + 5:]
    c = pl.program_id(1)
    n_rows = n_heads * t_new
    n_cols = PAGE_SIZE * n_heads

    @pl.when(c == 0)
    def _():
        row = lax.broadcasted_iota(jnp.int32, (n_rows, n_cols), 0)
        col = lax.broadcasted_iota(jnp.int32, (n_rows, n_cols), 1)
        same_head = (row // t_new) == (col % n_heads)
        mb_ref[...] = jnp.where(same_head, cncol_ref[...], NEG_INF)
        m_ref[...] = jnp.full_like(m_ref, NEG_INF)
        l_ref[...] = jnp.zeros_like(l_ref)
        acc_ref[...] = jnp.zeros_like(acc_ref)
        later_ref[...] = jnp.zeros_like(later_ref)

    q = q_ref[...]
    s_parts = []
    for i in range(n_pp):
        kp = k_refs[i][...].astype(BF16)
        s = lax.dot_general(q, kp, NT_DIMS, preferred_element_type=F32)
        bias = after_refs[i][...] + later_ref[...]
        later_ref[...] += total_refs[i][...]
        chunks = []
        for j in range(n_cols // LANES):
            cs = slice(j * LANES, (j + 1) * LANES)
            chunks.append(s[:, cs] * ATTN_SCALE + mb_ref[:, cs] + bias[j:j + 1, :])
        s_parts.append(jnp.concatenate(chunks, axis=1))
    m_prev = m_ref[...]
    m_new = m_prev
    for s in s_parts:
        m_new = jnp.maximum(m_new, jnp.max(s, axis=-1, keepdims=True))
    alpha = jnp.exp(m_prev - m_new)
    l_new = alpha * l_ref[...]
    acc = alpha * acc_ref[...]
    for i, s in enumerate(s_parts):
        p = jnp.exp(s - m_new)
        l_new = l_new + jnp.sum(p, axis=-1, keepdims=True)
        acc = acc + jnp.dot(p.astype(BF16), v_refs[i][...].astype(BF16),
                            preferred_element_type=F32)
    m_ref[...] = m_new
    l_ref[...] = l_new
    acc_ref[...] = acc

    @pl.when(c == pl.num_programs(1) - 1)
    def _():
        n_new = t_new * n_heads
        knp_ref[0:n_new, :] = kn_ref[...].astype(BF16)
        knp_ref[n_new:, :] = jnp.zeros((LANES - n_new, HEAD_DIM), BF16)
        vnp_ref[0:n_new, :] = vn_ref[...].astype(BF16)
        vnp_ref[n_new:, :] = jnp.zeros((LANES - n_new, HEAD_DIM), BF16)
        s = lax.dot_general(q, knp_ref[...], NT_DIMS, preferred_element_type=F32)
        row = lax.broadcasted_iota(jnp.int32, (n_rows, LANES), 0)
        col = lax.broadcasted_iota(jnp.int32, (n_rows, LANES), 1)
        valid = jnp.logical_and((row // t_new) == (col % n_heads),
                                (col // n_heads) <= (row % t_new))
        s = jnp.where(valid, s * ATTN_SCALE + cncol_ref[...] - cnrow_ref[...], NEG_INF)
        _softmax_step(s, vnp_ref[...], m_ref, l_ref, acc_ref)
        o_ref[...] = (acc_ref[...] / l_ref[...]).astype(BF16)


def _attn_sample(page_table, q_rows, k_pages, v_pages, lf_after, lf_total, k_new, v_new,
                 cn_row, cn_col, t_new, n_heads):
    n_seq, n_pages = page_table.shape
    n_rows = n_heads * t_new
    n_cols = PAGE_SIZE * n_heads
    n_sub = n_cols // LANES
    n_new = t_new * n_heads
    n_pp = 8
    steps = n_pages // n_pp

    def page_idx(b, c, pt, i):
        return pt[b, n_pages - 1 - (c * n_pp + i)]

    def page_spec(i):
        return pl.BlockSpec((None, n_cols, HEAD_DIM), lambda b, c, pt: (page_idx(b, c, pt, i), 0, 0))

    def lf_spec(i):
        return pl.BlockSpec((None, n_sub, LANES), lambda b, c, pt: (page_idx(b, c, pt, i), 0, 0))

    def seq_spec(r, w):
        return pl.BlockSpec((None, r, w), lambda b, c, pt: (b, 0, 0))

    return pl.pallas_call(
        functools.partial(_attn_sample_kernel, n_pp=n_pp, t_new=t_new, n_heads=n_heads),
        grid_spec=pltpu.PrefetchScalarGridSpec(
            num_scalar_prefetch=1,
            grid=(n_seq, steps),
            in_specs=([seq_spec(n_rows, HEAD_DIM)]
                      + [page_spec(i) for i in range(n_pp)] * 2
                      + [lf_spec(i) for i in range(n_pp)] * 2
                      + [seq_spec(n_new, HEAD_DIM), seq_spec(n_new, HEAD_DIM),
                         seq_spec(1, LANES), seq_spec(n_rows, 1)]),
            out_specs=seq_spec(n_rows, HEAD_DIM),
            scratch_shapes=[pltpu.VMEM((n_rows, n_cols), F32),
                            pltpu.VMEM((n_rows, 1), F32), pltpu.VMEM((n_rows, 1), F32),
                            pltpu.VMEM((n_rows, HEAD_DIM), F32),
                            pltpu.VMEM((LANES, HEAD_DIM), BF16),
                            pltpu.VMEM((LANES, HEAD_DIM), BF16),
                            pltpu.VMEM((n_sub, LANES), F32)],
        ),
        out_shape=jax.ShapeDtypeStruct((n_seq, n_rows, HEAD_DIM), BF16),
        compiler_params=_params(2),
        name="attn_sample",
    )(page_table, q_rows, *([k_pages] * n_pp), *([v_pages] * n_pp),
      *([lf_after] * n_pp), *([lf_total] * n_pp), k_new, v_new, cn_row, cn_col)


def _outproj_kernel(a_ref, p_ref, wa_ref, wp_ref, x_ref, ga_ref, o_ref, wabf_ref, wpbf_ref):
    @pl.when(pl.program_id(1) == 0)
    def _():
        wabf_ref[...] = wa_ref[...].astype(BF16)
        wpbf_ref[...] = wp_ref[...].astype(BF16)

    y = jnp.dot(a_ref[...], wabf_ref[...], preferred_element_type=F32)
    y = y + jnp.dot(p_ref[...], wpbf_ref[...], preferred_element_type=F32)
    for rows in _row_groups(x_ref.shape[0], ga_ref):
        o_ref[rows, :] = x_ref[rows, :] + ga_ref[...] * y[rows, :]


def _outproj(a, pm, w_out, x, mod, rows_per_batch, n_seq, ga_chunk):
    m, ka = a.shape
    kp = pm.shape[1]
    d = w_out.shape[1]
    tm, tn = min(m, 1024), 512
    tiles = rows_per_batch // tm
    ga_col0 = ga_chunk * (d // tn)
    return pl.pallas_call(
        _outproj_kernel,
        grid=(d // tn, m // tm),
        in_specs=[pl.BlockSpec((tm, ka), lambda j, i: (i, 0)),
                  pl.BlockSpec((tm, kp), lambda j, i: (i, 0)),
                  pl.BlockSpec((ka, tn), lambda j, i: (0, j)),
                  pl.BlockSpec((kp, tn), lambda j, i: (ka // kp, j)),
                  pl.BlockSpec((tm, tn), lambda j, i: (i, j)),
                  pl.BlockSpec(_mod_block(mod, tn, n_seq),
                               lambda j, i: _mod_index(mod, i // tiles, ga_col0 + j))],
        out_specs=pl.BlockSpec((tm, tn), lambda j, i: (i, j)),
        out_shape=jax.ShapeDtypeStruct((m, d), F32),
        scratch_shapes=[pltpu.VMEM((ka, tn), BF16), pltpu.VMEM((kp, tn), BF16)],
        compiler_params=_params(2),
        name="out_proj",
    )(a, pm, w_out, w_out, x, mod)


def _ffn_kernel(h_ref, wu_ref, wd_ref, x_ref, ga_ref, o_ref, hid_ref, *, n_up, tf):
    s = pl.program_id(1)

    @pl.when(s < n_up)
    def _():
        a = jnp.dot(h_ref[...], wu_ref[...], preferred_element_type=F32)
        col = pl.multiple_of(s * tf, tf)
        hid_ref[:, pl.ds(col, tf)] = jnp.square(jnp.maximum(a, 0.0)).astype(BF16)

    @pl.when(s >= n_up)
    def _():
        y = jnp.dot(hid_ref[...], wd_ref[...], preferred_element_type=F32)
        for rows in _row_groups(x_ref.shape[0], ga_ref):
            o_ref[rows, :] = x_ref[rows, :] + ga_ref[...] * y[rows, :]


def _ffn(h, x, w_up_bf, w_down_bf, mod, rows_per_batch, n_seq, ga_chunk):
    m, d = x.shape
    d_ff = w_up_bf.shape[1]
    tm, tf, tn = 512, 512, 256
    tiles = rows_per_batch // tm
    n_up, n_dn = d_ff // tf, d // tn
    ga_col0 = ga_chunk * n_dn

    def up_step(s):
        return jnp.minimum(s, n_up - 1)

    def dn_step(s):
        return jnp.maximum(s - n_up, 0)

    return pl.pallas_call(
        functools.partial(_ffn_kernel, n_up=n_up, tf=tf),
        grid=(m // tm, n_up + n_dn),
        in_specs=[pl.BlockSpec((tm, d), lambda i, s: (i, 0)),
                  pl.BlockSpec((d, tf), lambda i, s: (0, up_step(s))),
                  pl.BlockSpec((d_ff, tn), lambda i, s: (0, dn_step(s))),
                  pl.BlockSpec((tm, tn), lambda i, s: (i, dn_step(s))),
                  pl.BlockSpec(_mod_block(mod, tn, n_seq),
                               lambda i, s: _mod_index(mod, i // tiles, ga_col0 + dn_step(s)))],
        out_specs=pl.BlockSpec((tm, tn), lambda i, s: (i, dn_step(s))),
        out_shape=jax.ShapeDtypeStruct((m, d), F32),
        scratch_shapes=[pltpu.VMEM((tm, d_ff), BF16)],
        compiler_params=_params(2),
        name="ffn",
    )(h, w_up_bf, w_down_bf, x, mod)


def _pad_cols(w, n):
    return jnp.pad(w, ((0, 0), (0, n - w.shape[1])))


def kernel(x_prompt, x_sample, c_prompt, c_sample, cache_k, cache_v, cache_logf, state_pool,
           page_table, w_ada, b_ada, g_mix, w_in, b_f, w_pool, pool_scale, w_out, g_ffn,
           w_up, w_down, g_final):
    n_b, seq, d = x_prompt.shape
    n_s, t_new, _ = x_sample.shape
    assert w_ada.shape[0] == 1, "single trunk layer"
    l = 0
    n_heads = b_f.shape[1]
    attn_w = n_heads * HEAD_DIM
    pool_w = pool_scale.shape[1]
    n_phys = cache_k.shape[1]
    n_pages = page_table.shape[1]
    n_cols = PAGE_SIZE * n_heads
    SH1, SC1, GA1, SH2, SC2, GA2 = range(N_ADA)

    w_in_l = w_in[l].T
    gate_col = 3 * attn_w
    b_f_p = _pad_cols(b_f[l].reshape(1, n_heads), LANES)
    w_pool_bf = w_pool[l].astype(BF16)
    w_up_bf = w_up[l].astype(BF16)
    w_down_bf = w_down[l].astype(BF16)

    n_c = n_s + n_b
    c_rows = -(-n_c // 16) * 16
    c_all = jnp.pad(jnp.concatenate([c_sample, c_prompt], axis=0), ((0, c_rows - n_c), (0, 0)))
    mod_s = _modulation(c_all, w_ada[l], b_ada[l])
    mod_p = mod_s[n_s:n_c].reshape(n_b, 1, N_ADA * d)

    def trunk_front(x2, mod_x, rows_per_batch):
        h = _norm_mod(x2, g_mix[l], mod_x, rows_per_batch, n_s, SC1, SH1)
        (q,) = _proj(h, w_in_l, 0, attn_w, [BF16], "proj_q")
        (k,) = _proj(h, w_in_l, attn_w, attn_w, [F32], "proj_k")
        (v,) = _proj(h, w_in_l, 2 * attn_w, attn_w, [F32], "proj_v")
        (u,) = _proj(h, w_in_l, gate_col + n_heads, pool_w, [F32], "proj_u")
        return h, q, k, v, u

    def trunk_back(a, pm, x2, mod_x, rows_per_batch):
        x1 = _outproj(a, pm, w_out[l], x2, mod_x, rows_per_batch, n_s, GA1)
        h2 = _norm_mod(x1, g_ffn[l], mod_x, rows_per_batch, n_s, SC2, SH2)
        x3 = _ffn(h2, x1, w_up_bf, w_down_bf, mod_x, rows_per_batch, n_s, GA2)
        return _final_norm(x3, g_final)

    xp = x_prompt.reshape(n_b * seq, d)
    h, q, k_p, v_p, u_p = trunk_front(xp, mod_p, seq)
    logf_p, cum_p = _logf_prompt(h, w_in_l, gate_col, b_f_p, seq, n_heads)
    cum_t = cum_p.reshape(n_b, seq, n_heads).transpose(0, 2, 1)
    a_p = _attn_prompt(q, k_p, v_p, cum_t, cum_p, seq, n_heads)
    pm_p = _pool_prompt(u_p, w_pool_bf, pool_scale[l], seq)
    y_p = trunk_back(a_p, pm_p, xp, mod_p, seq)
    hist_p = u_p.reshape(n_b, seq, pool_w)[:, seq - POOL_HIST:]

    def seq_major(x2):
        return x2.reshape(t_new, n_s, x2.shape[-1]).transpose(1, 0, 2)

    xs = x_sample.transpose(1, 0, 2).reshape(t_new * n_s, d)
    h, q, k_s, v_s, u_s = trunk_front(xs, mod_s, t_new * n_s)
    logf_s, cn = _logf_sample(h, w_in_l, gate_col, b_f_p, n_s, n_heads)
    k_s, v_s, logf_s, cn = seq_major(k_s), seq_major(v_s), seq_major(logf_s), seq_major(cn)

    lf_after, lf_total = _page_suffix(
        cache_logf[l].reshape(n_phys, n_cols // LANES, LANES), n_heads)
    cn_row = _pad_cols(cn.reshape(n_s, t_new * n_heads), LANES).reshape(n_s, 1, LANES)
    cn_col = cn.transpose(0, 2, 1).reshape(n_s, n_heads * t_new, 1)
    q_rows = q.reshape(t_new, n_s, n_heads, HEAD_DIM).transpose(1, 2, 0, 3)
    q_rows = q_rows.reshape(n_s, n_heads * t_new, HEAD_DIM)
    a_s = _attn_sample(page_table, q_rows,
                       cache_k[l].reshape(n_phys, n_cols, HEAD_DIM),
                       cache_v[l].reshape(n_phys, n_cols, HEAD_DIM),
                       lf_after, lf_total,
                       k_s.reshape(n_s, t_new * n_heads, HEAD_DIM),
                       v_s.reshape(n_s, t_new * n_heads, HEAD_DIM),
                       cn_row, cn_col, t_new, n_heads)
    a_s = a_s.reshape(n_s, n_heads, t_new, HEAD_DIM).transpose(2, 0, 1, 3)
    a_s = a_s.reshape(t_new * n_s, attn_w)

    ext_t = jnp.concatenate([state_pool[l].transpose(1, 0, 2),
                             u_s.reshape(t_new, n_s, pool_w)], axis=0)
    pm_s = _pool_sample(ext_t, w_pool_bf, pool_scale[l], t_new).reshape(t_new * n_s, pool_w)
    y_s = trunk_back(a_s, pm_s, xs, mod_s, t_new * n_s)
    hist_s = ext_t[t_new:].transpose(1, 0, 2)

    kv_p = (1, n_b, seq, n_heads, HEAD_DIM)
    kv_s = (1, n_s, t_new, n_heads, HEAD_DIM)
    return (y_p.reshape(n_b, seq, d), seq_major(y_s),
            k_p.reshape(kv_p), v_p.reshape(kv_p), logf_p.reshape(1, n_b, seq, n_heads), hist_p[None],
            k_s.reshape(kv_s), v_s.reshape(kv_s), logf_s[None], hist_s[None])
```

```python
import functools

import jax
import jax.numpy as jnp
from jax import lax
from jax.experimental import pallas as pl
from jax.experimental.pallas import tpu as pltpu

F32 = jnp.float32
BF16 = jnp.bfloat16

HEAD_DIM = 128
POOL_WINDOWS = (2, 4, 8, 16)
POOL_HIST = max(POOL_WINDOWS) - 1
HALO = 16
RMS_EPS = 1e-6
N_ADA = 6
PAGE_SIZE = 128
ATTN_SCALE = HEAD_DIM ** -0.5
LANES = 128
V7X_VMEM_LIMIT = 56 * 1024 * 1024
NEG_INF = float("-inf")
LOG2E = 1.4426950408889634
NT_DIMS = (((1,), (1,)), ((), ()))


def _params(n_grid):
    return pltpu.CompilerParams(
        dimension_semantics=("arbitrary",) * n_grid, vmem_limit_bytes=V7X_VMEM_LIMIT)


def _log_sigmoid(z):
    return jnp.minimum(z, 0.0) - jnp.log1p(jnp.exp(-jnp.abs(z)))


def _rms_scale(x):
    return lax.rsqrt(jnp.mean(x * x, axis=-1, keepdims=True) + RMS_EPS)


def _mod_block(mod, tn, n_seq):
    return (n_seq, tn) if mod.ndim == 2 else (None, 1, tn)


def _mod_index(mod, batch_idx, col_blk):
    return (0, col_blk) if mod.ndim == 2 else (batch_idx, 0, col_blk)


def _row_groups(n_rows, mod_ref):
    group = n_rows if mod_ref.shape[0] == 1 else mod_ref.shape[0]
    return [slice(r, r + group) for r in range(0, n_rows, group)]


def _mod_kernel(c_ref, w_ref, b_ref, o_ref, s_ref):
    @pl.when(pl.program_id(0) == 0)
    def _():
        c = c_ref[...]
        s_ref[...] = (c * jax.nn.sigmoid(c)).astype(BF16)

    o_ref[...] = jnp.dot(s_ref[...], w_ref[...].astype(BF16),
                         preferred_element_type=F32) + b_ref[...]


def _modulation(c_all, w_ada, b_ada):
    rows, d = c_all.shape
    n = w_ada.shape[1]
    tn = 512
    return pl.pallas_call(
        _mod_kernel,
        grid=(n // tn,),
        in_specs=[pl.BlockSpec((rows, d), lambda j: (0, 0)),
                  pl.BlockSpec((d, tn), lambda j: (0, j)),
                  pl.BlockSpec((1, tn), lambda j: (0, j))],
        out_specs=pl.BlockSpec((rows, tn), lambda j: (0, j)),
        out_shape=jax.ShapeDtypeStruct((rows, n), F32),
        scratch_shapes=[pltpu.VMEM((rows, d), BF16)],
        compiler_params=_params(1),
        name="adaln_mod",
    )(c_all, w_ada, b_ada.reshape(1, n))


def _norm_kernel(x_ref, g_ref, sc_ref, sh_ref, o_ref):
    for rows in _row_groups(x_ref.shape[0], sc_ref):
        x = x_ref[rows, :]
        h = x * _rms_scale(x) * g_ref[...]
        o_ref[rows, :] = (h * (1.0 + sc_ref[...]) + sh_ref[...]).astype(BF16)


def _norm_mod(x, g, mod, rows_per_batch, n_seq, sc_chunk, sh_chunk):
    m, d = x.shape
    tm = 512
    tiles = rows_per_batch // tm

    def mod_spec(chunk):
        return pl.BlockSpec(_mod_block(mod, d, n_seq),
                            lambda i: _mod_index(mod, i // tiles, chunk))

    return pl.pallas_call(
        _norm_kernel,
        grid=(m // tm,),
        in_specs=[pl.BlockSpec((tm, d), lambda i: (i, 0)),
                  pl.BlockSpec((1, d), lambda i: (0, 0)),
                  mod_spec(sc_chunk), mod_spec(sh_chunk)],
        out_specs=pl.BlockSpec((tm, d), lambda i: (i, 0)),
        out_shape=jax.ShapeDtypeStruct((m, d), BF16),
        compiler_params=_params(1),
        name="norm_mod",
    )(x, g.reshape(1, d), mod, mod)


def _final_norm_kernel(x_ref, g_ref, o_ref):
    x = x_ref[...]
    o_ref[...] = x * _rms_scale(x) * g_ref[...]


def _final_norm(x, g):
    m, d = x.shape
    tm = 512
    return pl.pallas_call(
        _final_norm_kernel,
        grid=(m // tm,),
        in_specs=[pl.BlockSpec((tm, d), lambda i: (i, 0)),
                  pl.BlockSpec((1, d), lambda i: (0, 0))],
        out_specs=pl.BlockSpec((tm, d), lambda i: (i, 0)),
        out_shape=jax.ShapeDtypeStruct((m, d), F32),
        compiler_params=_params(1),
        name="final_norm",
    )(x, g.reshape(1, d))


def _proj_kernel(x_ref, w_ref, *rest, shift):
    if shift:
        wnext_ref, rest = rest[0], rest[1:]
    out_refs, wbf_ref = rest[:-1], rest[-1]
    tn = wbf_ref.shape[0]

    @pl.when(pl.program_id(1) == 0)
    def _():
        wbf_ref[0:tn - shift, :] = w_ref[shift:tn, :].astype(BF16)
        if shift:
            wbf_ref[tn - shift:tn, :] = wnext_ref[0:shift, :].astype(BF16)

    acc = lax.dot_general(x_ref[...], wbf_ref[...], NT_DIMS, preferred_element_type=F32)
    for o_ref in out_refs:
        o_ref[...] = acc.astype(o_ref.dtype)


def _proj(x, w_t, row0, n, out_dtypes, name):
    m, k = x.shape
    tm, tn = min(m, 1024), 512
    off, shift = row0 // tn, row0 % tn
    assert shift % 16 == 0 and shift <= LANES
    in_specs = [pl.BlockSpec((tm, k), lambda j, i: (i, 0)),
                pl.BlockSpec((tn, k), lambda j, i: (off + j, 0))]
    operands = [x, w_t]
    if shift:
        in_specs.append(pl.BlockSpec((LANES, k), lambda j, i: ((off + j + 1) * (tn // LANES), 0)))
        operands.append(w_t)
    return pl.pallas_call(
        functools.partial(_proj_kernel, shift=shift),
        grid=(n // tn, m // tm),
        in_specs=in_specs,
        out_specs=[pl.BlockSpec((tm, tn), lambda j, i: (i, j)) for _ in out_dtypes],
        out_shape=[jax.ShapeDtypeStruct((m, n), dt) for dt in out_dtypes],
        scratch_shapes=[pltpu.VMEM((tn, k), BF16)],
        compiler_params=_params(2),
        name=name,
    )(*operands)


def _logf_prompt_kernel(h_ref, w_ref, b_ref, lf_ref, cum_ref, carry_ref, *, tm, n_heads):
    @pl.when(pl.program_id(1) == 0)
    def _():
        carry_ref[...] = jnp.zeros_like(carry_ref)

    lf = _log_sigmoid(lax.dot_general(h_ref[...], w_ref[...].astype(BF16), NT_DIMS,
                                      preferred_element_type=F32) + b_ref[...])
    row = lax.broadcasted_iota(jnp.int32, (tm, tm), 0)
    col = lax.broadcasted_iota(jnp.int32, (tm, tm), 1)
    tri = (col <= row).astype(F32)
    cum = jnp.dot(tri, lf, preferred_element_type=F32,
                  precision=lax.Precision.HIGHEST) + carry_ref[...]
    carry_ref[...] = cum[tm - 1:tm, :]
    lf_ref[...] = lf[:, :n_heads]
    cum_ref[...] = cum[:, :n_heads]


def _logf_prompt(h, w, gate_col, b_f, seq, n_heads):
    m, d = h.shape
    tm = 512
    tiles = seq // tm
    assert gate_col % LANES == 0
    return pl.pallas_call(
        functools.partial(_logf_prompt_kernel, tm=tm, n_heads=n_heads),
        grid=(m // seq, tiles),
        in_specs=[pl.BlockSpec((tm, d), lambda b, t: (b * tiles + t, 0)),
                  pl.BlockSpec((LANES, d), lambda b, t: (gate_col // LANES, 0)),
                  pl.BlockSpec((1, LANES), lambda b, t: (0, 0))],
        out_specs=[pl.BlockSpec((tm, n_heads), lambda b, t: (b * tiles + t, 0))] * 2,
        out_shape=[jax.ShapeDtypeStruct((m, n_heads), F32)] * 2,
        scratch_shapes=[pltpu.VMEM((1, LANES), F32)],
        compiler_params=_params(2),
        name="logf_prompt",
    )(h, w, b_f)


def _logf_sample_kernel(h_ref, w_ref, b_ref, lf_ref, cum_ref, *, rows, n_seq, n_heads):
    lf = _log_sigmoid(lax.dot_general(h_ref[...], w_ref[...].astype(BF16), NT_DIMS,
                                      preferred_element_type=F32) + b_ref[...])
    row = lax.broadcasted_iota(jnp.int32, (rows, rows), 0)
    col = lax.broadcasted_iota(jnp.int32, (rows, rows), 1)
    tri = jnp.logical_and((row % n_seq) == (col % n_seq), col <= row).astype(F32)
    cum = jnp.dot(tri, lf, preferred_element_type=F32, precision=lax.Precision.HIGHEST)
    lf_ref[...] = lf[:, :n_heads]
    cum_ref[...] = cum[:, :n_heads]


def _logf_sample(h, w, gate_col, b_f, n_seq, n_heads):
    rows, d = h.shape
    assert gate_col % LANES == 0
    return pl.pallas_call(
        functools.partial(_logf_sample_kernel, rows=rows, n_seq=n_seq, n_heads=n_heads),
        grid=(1,),
        in_specs=[pl.BlockSpec((rows, d), lambda i: (0, 0)),
                  pl.BlockSpec((LANES, d), lambda i: (gate_col // LANES, 0)),
                  pl.BlockSpec((1, LANES), lambda i: (0, 0))],
        out_specs=[pl.BlockSpec((rows, n_heads), lambda i: (0, 0))] * 2,
        out_shape=[jax.ShapeDtypeStruct((rows, n_heads), F32)] * 2,
        compiler_params=_params(1),
        name="logf_sample",
    )(h, w, b_f)


def _page_suffix_kernel(x_ref, after_ref, total_ref, *, n_heads):
    x = x_ref[...]
    n_sub = x.shape[1]
    lane = lax.broadcasted_iota(jnp.int32, x.shape, 2)
    sub = lax.broadcasted_iota(jnp.int32, x.shape, 1)
    incl = x
    rows = x
    sh = n_heads
    while sh < LANES:
        incl = incl + jnp.where(lane + sh < LANES, pltpu.roll(incl, LANES - sh, axis=2), 0.0)
        rows = rows + pltpu.roll(rows, sh, axis=2)
        sh *= 2
    below = rows
    total = rows
    sh = 1
    while sh < n_sub:
        below = below + jnp.where(sub + sh < n_sub, pltpu.roll(below, n_sub - sh, axis=1), 0.0)
        total = total + pltpu.roll(total, sh, axis=1)
        sh *= 2
    after_ref[...] = (incl - x) + (below - rows)
    total_ref[...] = total


def _page_suffix(logf_pages, n_heads):
    n_phys, n_sub, _ = logf_pages.shape
    tp = 256
    blk = pl.BlockSpec((tp, n_sub, LANES), lambda i: (i, 0, 0))
    return pl.pallas_call(
        functools.partial(_page_suffix_kernel, n_heads=n_heads),
        grid=(n_phys // tp,),
        in_specs=[blk],
        out_specs=[blk, blk],
        out_shape=[jax.ShapeDtypeStruct(logf_pages.shape, F32)] * 2,
        compiler_params=_params(1),
        name="page_suffix",
    )(logf_pages)


def _pool_prompt_kernel(u_ref, halo_ref, wp_ref, ps_ref, o_ref, ext_ref, *, tm, group):
    t = pl.program_id(1)
    ext_ref[0:HALO, :] = jnp.where(t > 0, halo_ref[...], 0.0)
    ext_ref[HALO:HALO + tm, :] = u_ref[...]
    pos = t * tm + lax.broadcasted_iota(jnp.int32, (tm, 1), 0)
    for g, w in enumerate(POOL_WINDOWS):
        cs = slice(g * group, (g + 1) * group)
        cur = u_ref[:, cs]
        win = cur
        for j in range(1, w):
            win = win + ext_ref[HALO - j:HALO - j + tm, cs]
        cnt = jnp.minimum(pos + 1, w).astype(F32)
        d = win / cnt - cur
        y = jnp.dot(d.astype(BF16), wp_ref[g], preferred_element_type=F32) * ps_ref[:, cs]
        o_ref[:, cs] = y.astype(BF16)


def _pool_prompt(u, w_pool_bf, pool_scale, seq):
    m, width = u.shape
    n_groups, group, _ = w_pool_bf.shape
    tm = 256
    tiles = seq // tm
    return pl.pallas_call(
        functools.partial(_pool_prompt_kernel, tm=tm, group=group),
        grid=(m // seq, tiles),
        in_specs=[pl.BlockSpec((tm, width), lambda b, t: (b * tiles + t, 0)),
                  pl.BlockSpec((HALO, width),
                               lambda b, t: (jnp.maximum((b * tiles + t) * (tm // HALO) - 1, 0), 0)),
                  pl.BlockSpec((n_groups, group, group), lambda b, t: (0, 0, 0)),
                  pl.BlockSpec((1, width), lambda b, t: (0, 0))],
        out_specs=pl.BlockSpec((tm, width), lambda b, t: (b * tiles + t, 0)),
        out_shape=jax.ShapeDtypeStruct((m, width), BF16),
        scratch_shapes=[pltpu.VMEM((HALO + tm, width), F32)],
        compiler_params=_params(2),
        name="pool_prompt",
    )(u, u, w_pool_bf, pool_scale.reshape(1, width))


def _pool_sample_kernel(ext_ref, wp_ref, ps_ref, o_ref, *, t_new, n_seq):
    g = pl.program_id(0)
    for gi, w in enumerate(POOL_WINDOWS):
        @pl.when(g == gi)
        def _(w=w):
            ds = []
            for t in range(t_new):
                cur = ext_ref[POOL_HIST + t]
                win = cur
                for j in range(1, w):
                    win = win + ext_ref[POOL_HIST + t - j]
                ds.append(win / float(w) - cur)
            d = jnp.concatenate(ds, axis=0).astype(BF16)
            y = jnp.dot(d, wp_ref[...], preferred_element_type=F32) * ps_ref[...]
            for t in range(t_new):
                o_ref[t] = y[t * n_seq:(t + 1) * n_seq].astype(BF16)


def _pool_sample(ext_t, w_pool_bf, pool_scale, t_new):
    rows, n_seq, width = ext_t.shape
    n_groups, group, _ = w_pool_bf.shape
    return pl.pallas_call(
        functools.partial(_pool_sample_kernel, t_new=t_new, n_seq=n_seq),
        grid=(n_groups,),
        in_specs=[pl.BlockSpec((rows, n_seq, group), lambda g: (0, 0, g)),
                  pl.BlockSpec((None, group, group), lambda g: (g, 0, 0)),
                  pl.BlockSpec((1, group), lambda g: (0, g))],
        out_specs=pl.BlockSpec((t_new, n_seq, group), lambda g: (0, 0, g)),
        out_shape=jax.ShapeDtypeStruct((t_new, n_seq, width), BF16),
        compiler_params=_params(1),
        name="pool_sample",
    )(ext_t, w_pool_bf, pool_scale.reshape(1, width))


def _softmax_step(s, v, m_ref, l_ref, acc_ref):
    m_prev = m_ref[...]
    m_new = jnp.maximum(m_prev, jnp.max(s, axis=-1, keepdims=True))
    alpha = jnp.exp(m_prev - m_new)
    p = jnp.exp(s - m_new)
    l_ref[...] = alpha * l_ref[...] + jnp.sum(p, axis=-1, keepdims=True)
    acc_ref[...] = alpha * acc_ref[...] + jnp.dot(p.astype(BF16), v, preferred_element_type=F32)
    m_ref[...] = m_new


def _attn_prompt_kernel(q_ref, kf_ref, vf_ref, frow_ref, fcol_ref, o_ref, k_ref, v_ref,
                        *, seq, tq, n_heads):
    h = pl.program_id(1)
    k_ref[...] = kf_ref[...].astype(BF16)
    v_ref[...] = vf_ref[...].astype(BF16)
    lane = lax.broadcasted_iota(jnp.int32, (seq, n_heads), 1)
    fcol_all = jnp.sum(jnp.where(lane == h, fcol_ref[...], 0.0), axis=1, keepdims=True) * LOG2E
    frow_all = frow_ref[pl.ds(h, 1), :] * LOG2E
    row = lax.broadcasted_iota(jnp.int32, (tq, tq), 0)
    col = lax.broadcasted_iota(jnp.int32, (tq, tq), 1)
    causal = col <= row

    def scores(q, lo, hi):
        s = lax.dot_general(q, k_ref[lo:hi, :], NT_DIMS, preferred_element_type=F32)
        return s * (ATTN_SCALE * LOG2E) - frow_all[:, lo:hi]

    for qi in range(seq // tq):
        lo, hi = qi * tq, (qi + 1) * tq
        q = q_ref[lo:hi, :]
        fcol = fcol_all[lo:hi]
        t_diag = jnp.where(causal, scores(q, lo, hi), NEG_INF)
        t_max = jnp.max(t_diag, axis=-1, keepdims=True)
        if qi > 0:
            t_past = scores(q, 0, lo)
            t_max = jnp.maximum(t_max, jnp.max(t_past, axis=-1, keepdims=True))
        m = t_max + fcol
        shift = fcol - m
        if qi > 0:
            p = jnp.exp2(t_past + shift)
            l = jnp.sum(p, axis=-1, keepdims=True)
            acc = jnp.dot(p.astype(BF16), v_ref[0:lo, :], preferred_element_type=F32)
        p = jnp.exp2(t_diag + shift)
        if qi > 0:
            l = l + jnp.sum(p, axis=-1, keepdims=True)
            acc = acc + jnp.dot(p.astype(BF16), v_ref[lo:hi, :], preferred_element_type=F32)
        else:
            l = jnp.sum(p, axis=-1, keepdims=True)
            acc = jnp.dot(p.astype(BF16), v_ref[lo:hi, :], preferred_element_type=F32)
        o_ref[lo:hi, :] = (acc / l).astype(BF16)


def _attn_prompt(q, k, v, cum_t, cum, seq, n_heads):
    m, width = q.shape
    tq = 256
    blk = pl.BlockSpec((seq, HEAD_DIM), lambda b, h: (b, h))
    return pl.pallas_call(
        functools.partial(_attn_prompt_kernel, seq=seq, tq=tq, n_heads=n_heads),
        grid=(m // seq, n_heads),
        in_specs=[blk, blk, blk,
                  pl.BlockSpec((None, n_heads, seq), lambda b, h: (b, 0, 0)),
                  pl.BlockSpec((seq, n_heads), lambda b, h: (b, 0))],
        out_specs=blk,
        out_shape=jax.ShapeDtypeStruct((m, width), BF16),
        scratch_shapes=[pltpu.VMEM((seq, HEAD_DIM), BF16), pltpu.VMEM((seq, HEAD_DIM), BF16)],
        compiler_params=_params(2),
        name="attn_prompt",
    )(q, k, v, cum_t, cum)


def _attn_sample_kernel(pt_ref, q_ref, *refs, n_pp, t_new, n_heads):
    k_refs = refs[:n_pp]
    v_refs = refs[n_pp:2 * n_pp]
    after_refs = refs[2 * n_pp:3 * n_pp]
    total_refs = refs[3 * n_pp:4 * n_pp]
    kn_ref, vn_ref, cnrow_ref, cncol_ref, o_ref = refs[4 * n_pp:4 * n_pp + 5]
    mb_ref, m_ref, l_ref, acc_ref, knp_ref, vnp_ref, later_ref = refs[4 * n_pp + 5:]
    c = pl.program_id(1)
    n_rows = n_heads * t_new
    n_cols = PAGE_SIZE * n_heads

    @pl.when(c == 0)
    def _():
        row = lax.broadcasted_iota(jnp.int32, (n_rows, n_cols), 0)
        col = lax.broadcasted_iota(jnp.int32, (n_rows, n_cols), 1)
        same_head = (row // t_new) == (col % n_heads)
        mb_ref[...] = jnp.where(same_head, cncol_ref[...], NEG_INF)
        m_ref[...] = jnp.full_like(m_ref, NEG_INF)
        l_ref[...] = jnp.zeros_like(l_ref)
        acc_ref[...] = jnp.zeros_like(acc_ref)
        later_ref[...] = jnp.zeros_like(later_ref)

    q = q_ref[...]
    s_parts = []
    for i in range(n_pp):
        kp = k_refs[i][...].astype(BF16)
        s = lax.dot_general(q, kp, NT_DIMS, preferred_element_type=F32)
        bias = after_refs[i][...] + later_ref[...]
        later_ref[...] += total_refs[i][...]
        chunks = []
        for j in range(n_cols // LANES):
            cs = slice(j * LANES, (j + 1) * LANES)
            chunks.append(s[:, cs] * ATTN_SCALE + mb_ref[:, cs] + bias[j:j + 1, :])
        s_parts.append(jnp.concatenate(chunks, axis=1))
    m_prev = m_ref[...]
    m_new = m_prev
    for s in s_parts:
        m_new = jnp.maximum(m_new, jnp.max(s, axis=-1, keepdims=True))
    alpha = jnp.exp(m_prev - m_new)
    l_new = alpha * l_ref[...]
    acc = alpha * acc_ref[...]
    for i, s in enumerate(s_parts):
        p = jnp.exp(s - m_new)
        l_new = l_new + jnp.sum(p, axis=-1, keepdims=True)
        acc = acc + jnp.dot(p.astype(BF16), v_refs[i][...].astype(BF16),
                            preferred_element_type=F32)
    m_ref[...] = m_new
    l_ref[...] = l_new
    acc_ref[...] = acc

    @pl.when(c == pl.num_programs(1) - 1)
    def _():
        n_new = t_new * n_heads
        knp_ref[0:n_new, :] = kn_ref[...].astype(BF16)
        knp_ref[n_new:, :] = jnp.zeros((LANES - n_new, HEAD_DIM), BF16)
        vnp_ref[0:n_new, :] = vn_ref[...].astype(BF16)
        vnp_ref[n_new:, :] = jnp.zeros((LANES - n_new, HEAD_DIM), BF16)
        s = lax.dot_general(q, knp_ref[...], NT_DIMS, preferred_element_type=F32)
        row = lax.broadcasted_iota(jnp.int32, (n_rows, LANES), 0)
        col = lax.broadcasted_iota(jnp.int32, (n_rows, LANES), 1)
        valid = jnp.logical_and((row // t_new) == (col % n_heads),
                                (col // n_heads) <= (row % t_new))
        s = jnp.where(valid, s * ATTN_SCALE + cncol_ref[...] - cnrow_ref[...], NEG_INF)
        _softmax_step(s, vnp_ref[...], m_ref, l_ref, acc_ref)
        o_ref[...] = (acc_ref[...] / l_ref[...]).astype(BF16)


def _attn_sample(page_table, q_rows, k_pages, v_pages, lf_after, lf_total, k_new, v_new,
                 cn_row, cn_col, t_new, n_heads):
    n_seq, n_pages = page_table.shape
    n_rows = n_heads * t_new
    n_cols = PAGE_SIZE * n_heads
    n_sub = n_cols // LANES
    n_new = t_new * n_heads
    n_pp = 8
    steps = n_pages // n_pp

    def page_idx(b, c, pt, i):
        return pt[b, n_pages - 1 - (c * n_pp + i)]

    def page_spec(i):
        return pl.BlockSpec((None, n_cols, HEAD_DIM), lambda b, c, pt: (page_idx(b, c, pt, i), 0, 0))

    def lf_spec(i):
        return pl.BlockSpec((None, n_sub, LANES), lambda b, c, pt: (page_idx(b, c, pt, i), 0, 0))

    def seq_spec(r, w):
        return pl.BlockSpec((None, r, w), lambda b, c, pt: (b, 0, 0))

    return pl.pallas_call(
        functools.partial(_attn_sample_kernel, n_pp=n_pp, t_new=t_new, n_heads=n_heads),
        grid_spec=pltpu.PrefetchScalarGridSpec(
            num_scalar_prefetch=1,
            grid=(n_seq, steps),
            in_specs=([seq_spec(n_rows, HEAD_DIM)]
                      + [page_spec(i) for i in range(n_pp)] * 2
                      + [lf_spec(i) for i in range(n_pp)] * 2
                      + [seq_spec(n_new, HEAD_DIM), seq_spec(n_new, HEAD_DIM),
                         seq_spec(1, LANES), seq_spec(n_rows, 1)]),
            out_specs=seq_spec(n_rows, HEAD_DIM),
            scratch_shapes=[pltpu.VMEM((n_rows, n_cols), F32),
                            pltpu.VMEM((n_rows, 1), F32), pltpu.VMEM((n_rows, 1), F32),
                            pltpu.VMEM((n_rows, HEAD_DIM), F32),
                            pltpu.VMEM((LANES, HEAD_DIM), BF16),
                            pltpu.VMEM((LANES, HEAD_DIM), BF16),
                            pltpu.VMEM((n_sub, LANES), F32)],
        ),
        out_shape=jax.ShapeDtypeStruct((n_seq, n_rows, HEAD_DIM), BF16),
        compiler_params=_params(2),
        name="attn_sample",
    )(page_table, q_rows, *([k_pages] * n_pp), *([v_pages] * n_pp),
      *([lf_after] * n_pp), *([lf_total] * n_pp), k_new, v_new, cn_row, cn_col)


def _outproj_kernel(a_ref, p_ref, wa_ref, wp_ref, x_ref, ga_ref, o_ref, wabf_ref, wpbf_ref):
    @pl.when(pl.program_id(1) == 0)
    def _():
        wabf_ref[...] = wa_ref[...].astype(BF16)
        wpbf_ref[...] = wp_ref[...].astype(BF16)

    y = jnp.dot(a_ref[...], wabf_ref[...], preferred_element_type=F32)
    y = y + jnp.dot(p_ref[...], wpbf_ref[...], preferred_element_type=F32)
    for rows in _row_groups(x_ref.shape[0], ga_ref):
        o_ref[rows, :] = x_ref[rows, :] + ga_ref[...] * y[rows, :]


def _outproj(a, pm, w_out, x, mod, rows_per_batch, n_seq, ga_chunk):
    m, ka = a.shape
    kp = pm.shape[1]
    d = w_out.shape[1]
    tm, tn = min(m, 1024), 512
    tiles = rows_per_batch // tm
    ga_col0 = ga_chunk * (d // tn)
    return pl.pallas_call(
        _outproj_kernel,
        grid=(d // tn, m // tm),
        in_specs=[pl.BlockSpec((tm, ka), lambda j, i: (i, 0)),
                  pl.BlockSpec((tm, kp), lambda j, i: (i, 0)),
                  pl.BlockSpec((ka, tn), lambda j, i: (0, j)),
                  pl.BlockSpec((kp, tn), lambda j, i: (ka // kp, j)),
                  pl.BlockSpec((tm, tn), lambda j, i: (i, j)),
                  pl.BlockSpec(_mod_block(mod, tn, n_seq),
                               lambda j, i: _mod_index(mod, i // tiles, ga_col0 + j))],
        out_specs=pl.BlockSpec((tm, tn), lambda j, i: (i, j)),
        out_shape=jax.ShapeDtypeStruct((m, d), F32),
        scratch_shapes=[pltpu.VMEM((ka, tn), BF16), pltpu.VMEM((kp, tn), BF16)],
        compiler_params=_params(2),
        name="out_proj",
    )(a, pm, w_out, w_out, x, mod)


def _ffn_kernel(h_ref, wu_ref, wd_ref, x_ref, ga_ref, o_ref, hid_ref, *, n_up, tf):
    s = pl.program_id(1)

    @pl.when(s < n_up)
    def _():
        a = jnp.dot(h_ref[...], wu_ref[...], preferred_element_type=F32)
        col = pl.multiple_of(s * tf, tf)
        hid_ref[:, pl.ds(col, tf)] = jnp.square(jnp.maximum(a, 0.0)).astype(BF16)

    @pl.when(s >= n_up)
    def _():
        y = jnp.dot(hid_ref[...], wd_ref[...], preferred_element_type=F32)
        for rows in _row_groups(x_ref.shape[0], ga_ref):
            o_ref[rows, :] = x_ref[rows, :] + ga_ref[...] * y[rows, :]


def _ffn(h, x, w_up_bf, w_down_bf, mod, rows_per_batch, n_seq, ga_chunk):
    m, d = x.shape
    d_ff = w_up_bf.shape[1]
    tm, tf, tn = 512, 512, 256
    tiles = rows_per_batch // tm
    n_up, n_dn = d_ff // tf, d // tn
    ga_col0 = ga_chunk * n_dn

    def up_step(s):
        return jnp.minimum(s, n_up - 1)

    def dn_step(s):
        return jnp.maximum(s - n_up, 0)

    return pl.pallas_call(
        functools.partial(_ffn_kernel, n_up=n_up, tf=tf),
        grid=(m // tm, n_up + n_dn),
        in_specs=[pl.BlockSpec((tm, d), lambda i, s: (i, 0)),
                  pl.BlockSpec((d, tf), lambda i, s: (0, up_step(s))),
                  pl.BlockSpec((d_ff, tn), lambda i, s: (0, dn_step(s))),
                  pl.BlockSpec((tm, tn), lambda i, s: (i, dn_step(s))),
                  pl.BlockSpec(_mod_block(mod, tn, n_seq),
                               lambda i, s: _mod_index(mod, i // tiles, ga_col0 + dn_step(s)))],
        out_specs=pl.BlockSpec((tm, tn), lambda i, s: (i, dn_step(s))),
        out_shape=jax.ShapeDtypeStruct((m, d), F32),
        scratch_shapes=[pltpu.VMEM((tm, d_ff), BF16)],
        compiler_params=_params(2),
        name="ffn",
    )(h, w_up_bf, w_down_bf, x, mod)


def _pad_cols(w, n):
    return jnp.pad(w, ((0, 0), (0, n - w.shape[1])))


def kernel(x_prompt, x_sample, c_prompt, c_sample, cache_k, cache_v, cache_logf, state_pool,
           page_table, w_ada, b_ada, g_mix, w_in, b_f, w_pool, pool_scale, w_out, g_ffn,
           w_up, w_down, g_final):
    n_b, seq, d = x_prompt.shape
    n_s, t_new, _ = x_sample.shape
    assert w_ada.shape[0] == 1, "single trunk layer"
    l = 0
    n_heads = b_f.shape[1]
    attn_w = n_heads * HEAD_DIM
    pool_w = pool_scale.shape[1]
    n_phys = cache_k.shape[1]
    n_pages = page_table.shape[1]
    n_cols = PAGE_SIZE * n_heads
    SH1, SC1, GA1, SH2, SC2, GA2 = range(N_ADA)

    w_in_l = w_in[l].T
    gate_col = 3 * attn_w
    b_f_p = _pad_cols(b_f[l].reshape(1, n_heads), LANES)
    w_pool_bf = w_pool[l].astype(BF16)
    w_up_bf = w_up[l].astype(BF16)
    w_down_bf = w_down[l].astype(BF16)

    n_c = n_s + n_b
    c_rows = -(-n_c // 16) * 16
    c_all = jnp.pad(jnp.concatenate([c_sample, c_prompt], axis=0), ((0, c_rows - n_c), (0, 0)))
    mod_s = _modulation(c_all, w_ada[l], b_ada[l])
    mod_p = mod_s[n_s:n_c].reshape(n_b, 1, N_ADA * d)

    def trunk_front(x2, mod_x, rows_per_batch):
        h = _norm_mod(x2, g_mix[l], mod_x, rows_per_batch, n_s, SC1, SH1)
        (q,) = _proj(h, w_in_l, 0, attn_w, [BF16], "proj_q")
        (k,) = _proj(h, w_in_l, attn_w, attn_w, [F32], "proj_k")
        (v,) = _proj(h, w_in_l, 2 * attn_w, attn_w, [F32], "proj_v")
        (u,) = _proj(h, w_in_l, gate_col + n_heads, pool_w, [F32], "proj_u")
        return h, q, k, v, u

    def trunk_back(a, pm, x2, mod_x, rows_per_batch):
        x1 = _outproj(a, pm, w_out[l], x2, mod_x, rows_per_batch, n_s, GA1)
        h2 = _norm_mod(x1, g_ffn[l], mod_x, rows_per_batch, n_s, SC2, SH2)
        x3 = _ffn(h2, x1, w_up_bf, w_down_bf, mod_x, rows_per_batch, n_s, GA2)
        return _final_norm(x3, g_final)

    xp = x_prompt.reshape(n_b * seq, d)
    h, q, k_p, v_p, u_p = trunk_front(xp, mod_p, seq)
    logf_p, cum_p = _logf_prompt(h, w_in_l, gate_col, b_f_p, seq, n_heads)
    cum_t = cum_p.reshape(n_b, seq, n_heads).transpose(0, 2, 1)
    a_p = _attn_prompt(q, k_p, v_p, cum_t, cum_p, seq, n_heads)
    pm_p = _pool_prompt(u_p, w_pool_bf, pool_scale[l], seq)
    y_p = trunk_back(a_p, pm_p, xp, mod_p, seq)
    hist_p = u_p.reshape(n_b, seq, pool_w)[:, seq - POOL_HIST:]

    def seq_major(x2):
        return x2.reshape(t_new, n_s, x2.shape[-1]).transpose(1, 0, 2)

    xs = x_sample.transpose(1, 0, 2).reshape(t_new * n_s, d)
    h, q, k_s, v_s, u_s = trunk_front(xs, mod_s, t_new * n_s)
    logf_s, cn = _logf_sample(h, w_in_l, gate_col, b_f_p, n_s, n_heads)
    k_s, v_s, logf_s, cn = seq_major(k_s), seq_major(v_s), seq_major(logf_s), seq_major(cn)

    lf_after, lf_total = _page_suffix(
        cache_logf[l].reshape(n_phys, n_cols // LANES, LANES), n_heads)
    cn_row = _pad_cols(cn.reshape(n_s, t_new * n_heads), LANES).reshape(n_s, 1, LANES)
    cn_col = cn.transpose(0, 2, 1).reshape(n_s, n_heads * t_new, 1)
    q_rows = q.reshape(t_new, n_s, n_heads, HEAD_DIM).transpose(1, 2, 0, 3)
    q_rows = q_rows.reshape(n_s, n_heads * t_new, HEAD_DIM)
    a_s = _attn_sample(page_table, q_rows,
                       cache_k[l].reshape(n_phys, n_cols, HEAD_DIM),
                       cache_v[l].reshape(n_phys, n_cols, HEAD_DIM),
                       lf_after, lf_total,
                       k_s.reshape(n_s, t_new * n_heads, HEAD_DIM),
                       v_s.reshape(n_s, t_new * n_heads, HEAD_DIM),
                       cn_row, cn_col, t_new, n_heads)
    a_s = a_s.reshape(n_s, n_heads, t_new, HEAD_DIM).transpose(2, 0, 1, 3)
    a_s = a_s.reshape(t_new * n_s, attn_w)

    ext_t = jnp.concatenate([state_pool[l].transpose(1, 0, 2),
                             u_s.reshape(t_new, n_s, pool_w)], axis=0)
    pm_s = _pool_sample(ext_t, w_pool_bf, pool_scale[l], t_new).reshape(t_new * n_s, pool_w)
    y_s = trunk_back(a_s, pm_s, xs, mod_s, t_new * n_s)
    hist_s = ext_t[t_new:].transpose(1, 0, 2)

    kv_p = (1, n_b, seq, n_heads, HEAD_DIM)
    kv_s = (1, n_s, t_new, n_heads, HEAD_DIM)
    return (y_p.reshape(n_b, seq, d), seq_major(y_s),
            k_p.reshape(kv_p), v_p.reshape(kv_p), logf_p.reshape(1, n_b, seq, n_heads), hist_p[None],
            k_s.reshape(kv_s), v_s.reshape(kv_s), logf_s[None], hist_s[None])
```

```python
import functools

import jax
import jax.numpy as jnp
from jax import lax
from jax.experimental import pallas as pl
from jax.experimental.pallas import tpu as pltpu

F32 = jnp.float32
BF16 = jnp.bfloat16

HEAD_DIM = 128
POOL_WINDOWS = (2, 4, 8, 16)
POOL_HIST = max(POOL_WINDOWS) - 1
HALO = 16
RMS_EPS = 1e-6
N_ADA = 6
PAGE_SIZE = 128
ATTN_SCALE = HEAD_DIM ** -0.5
LANES = 128
V7X_VMEM_LIMIT = 56 * 1024 * 1024
NEG_INF = float("-inf")
LOG2E = 1.4426950408889634
PAGE_RING = 4
NT_DIMS = (((1,), (1,)), ((), ()))


def _params(n_grid):
    return pltpu.CompilerParams(
        dimension_semantics=("arbitrary",) * n_grid, vmem_limit_bytes=V7X_VMEM_LIMIT)


def _log_sigmoid(z):
    return jnp.minimum(z, 0.0) - jnp.log1p(jnp.exp(-jnp.abs(z)))


def _rms_scale(x):
    return lax.rsqrt(jnp.mean(x * x, axis=-1, keepdims=True) + RMS_EPS)


def _mod_block(mod, tn, n_seq):
    return (n_seq, tn) if mod.ndim == 2 else (None, 1, tn)


def _mod_index(mod, batch_idx, col_blk):
    return (0, col_blk) if mod.ndim == 2 else (batch_idx, 0, col_blk)


def _row_groups(n_rows, mod_ref):
    group = n_rows if mod_ref.shape[0] == 1 else mod_ref.shape[0]
    return [slice(r, r + group) for r in range(0, n_rows, group)]


def _mod_kernel(c_ref, w_ref, b_ref, o_ref, s_ref):
    @pl.when(pl.program_id(0) == 0)
    def _():
        c = c_ref[...]
        s_ref[...] = (c * jax.nn.sigmoid(c)).astype(BF16)

    o_ref[...] = jnp.dot(s_ref[...], w_ref[...].astype(BF16),
                         preferred_element_type=F32) + b_ref[...]


def _modulation(c_all, w_ada, b_ada):
    rows, d = c_all.shape
    n = w_ada.shape[1]
    tn = 512
    return pl.pallas_call(
        _mod_kernel,
        grid=(n // tn,),
        in_specs=[pl.BlockSpec((rows, d), lambda j: (0, 0)),
                  pl.BlockSpec((d, tn), lambda j: (0, j)),
                  pl.BlockSpec((1, tn), lambda j: (0, j))],
        out_specs=pl.BlockSpec((rows, tn), lambda j: (0, j)),
        out_shape=jax.ShapeDtypeStruct((rows, n), F32),
        scratch_shapes=[pltpu.VMEM((rows, d), BF16)],
        compiler_params=_params(1),
        name="adaln_mod",
    )(c_all, w_ada, b_ada.reshape(1, n))


def _norm_kernel(x_ref, g_ref, sc_ref, sh_ref, o_ref):
    for rows in _row_groups(x_ref.shape[0], sc_ref):
        x = x_ref[rows, :]
        h = x * _rms_scale(x) * g_ref[...]
        o_ref[rows, :] = (h * (1.0 + sc_ref[...]) + sh_ref[...]).astype(BF16)


def _norm_mod(x, g, mod, rows_per_batch, n_seq, sc_chunk, sh_chunk):
    m, d = x.shape
    tm = 512
    tiles = rows_per_batch // tm

    def mod_spec(chunk):
        return pl.BlockSpec(_mod_block(mod, d, n_seq),
                            lambda i: _mod_index(mod, i // tiles, chunk))

    return pl.pallas_call(
        _norm_kernel,
        grid=(m // tm,),
        in_specs=[pl.BlockSpec((tm, d), lambda i: (i, 0)),
                  pl.BlockSpec((1, d), lambda i: (0, 0)),
                  mod_spec(sc_chunk), mod_spec(sh_chunk)],
        out_specs=pl.BlockSpec((tm, d), lambda i: (i, 0)),
        out_shape=jax.ShapeDtypeStruct((m, d), BF16),
        compiler_params=_params(1),
        name="norm_mod",
    )(x, g.reshape(1, d), mod, mod)


def _final_norm_kernel(x_ref, g_ref, o_ref):
    x = x_ref[...]
    o_ref[...] = x * _rms_scale(x) * g_ref[...]


def _final_norm(x, g):
    m, d = x.shape
    tm = 512
    return pl.pallas_call(
        _final_norm_kernel,
        grid=(m // tm,),
        in_specs=[pl.BlockSpec((tm, d), lambda i: (i, 0)),
                  pl.BlockSpec((1, d), lambda i: (0, 0))],
        out_specs=pl.BlockSpec((tm, d), lambda i: (i, 0)),
        out_shape=jax.ShapeDtypeStruct((m, d), F32),
        compiler_params=_params(1),
        name="final_norm",
    )(x, g.reshape(1, d))


def _proj_kernel(x_ref, w_ref, *rest, shift):
    if shift:
        wnext_ref, rest = rest[0], rest[1:]
    out_refs, wbf_ref = rest[:-1], rest[-1]
    tn = wbf_ref.shape[0]

    @pl.when(pl.program_id(1) == 0)
    def _():
        wbf_ref[0:tn - shift, :] = w_ref[shift:tn, :].astype(BF16)
        if shift:
            wbf_ref[tn - shift:tn, :] = wnext_ref[0:shift, :].astype(BF16)

    acc = lax.dot_general(x_ref[...], wbf_ref[...], NT_DIMS, preferred_element_type=F32)
    for o_ref in out_refs:
        o_ref[...] = acc.astype(o_ref.dtype)


def _proj(x, w_t, row0, n, out_dtypes, name):
    m, k = x.shape
    tm, tn = min(m, 1024), 512
    off, shift = row0 // tn, row0 % tn
    assert shift % 16 == 0 and shift <= LANES
    in_specs = [pl.BlockSpec((tm, k), lambda j, i: (i, 0)),
                pl.BlockSpec((tn, k), lambda j, i: (off + j, 0))]
    operands = [x, w_t]
    if shift:
        in_specs.append(pl.BlockSpec((LANES, k), lambda j, i: ((off + j + 1) * (tn // LANES), 0)))
        operands.append(w_t)
    return pl.pallas_call(
        functools.partial(_proj_kernel, shift=shift),
        grid=(n // tn, m // tm),
        in_specs=in_specs,
        out_specs=[pl.BlockSpec((tm, tn), lambda j, i: (i, j)) for _ in out_dtypes],
        out_shape=[jax.ShapeDtypeStruct((m, n), dt) for dt in out_dtypes],
        scratch_shapes=[pltpu.VMEM((tn, k), BF16)],
        compiler_params=_params(2),
        name=name,
    )(*operands)


def _logf_prompt_kernel(h_ref, w_ref, b_ref, lf_ref, cum_ref, carry_ref, *, tm, n_heads):
    @pl.when(pl.program_id(1) == 0)
    def _():
        carry_ref[...] = jnp.zeros_like(carry_ref)

    lf = _log_sigmoid(lax.dot_general(h_ref[...], w_ref[...].astype(BF16), NT_DIMS,
                                      preferred_element_type=F32) + b_ref[...])
    row = lax.broadcasted_iota(jnp.int32, (tm, tm), 0)
    col = lax.broadcasted_iota(jnp.int32, (tm, tm), 1)
    tri = (col <= row).astype(F32)
    cum = jnp.dot(tri, lf, preferred_element_type=F32,
                  precision=lax.Precision.HIGHEST) + carry_ref[...]
    carry_ref[...] = cum[tm - 1:tm, :]
    lf_ref[...] = lf[:, :n_heads]
    cum_ref[...] = cum[:, :n_heads]


def _logf_prompt(h, w, gate_col, b_f, seq, n_heads):
    m, d = h.shape
    tm = 512
    tiles = seq // tm
    assert gate_col % LANES == 0
    return pl.pallas_call(
        functools.partial(_logf_prompt_kernel, tm=tm, n_heads=n_heads),
        grid=(m // seq, tiles),
        in_specs=[pl.BlockSpec((tm, d), lambda b, t: (b * tiles + t, 0)),
                  pl.BlockSpec((LANES, d), lambda b, t: (gate_col // LANES, 0)),
                  pl.BlockSpec((1, LANES), lambda b, t: (0, 0))],
        out_specs=[pl.BlockSpec((tm, n_heads), lambda b, t: (b * tiles + t, 0))] * 2,
        out_shape=[jax.ShapeDtypeStruct((m, n_heads), F32)] * 2,
        scratch_shapes=[pltpu.VMEM((1, LANES), F32)],
        compiler_params=_params(2),
        name="logf_prompt",
    )(h, w, b_f)


def _logf_sample_kernel(h_ref, w_ref, b_ref, lf_ref, cum_ref, *, rows, n_seq, n_heads):
    lf = _log_sigmoid(lax.dot_general(h_ref[...], w_ref[...].astype(BF16), NT_DIMS,
                                      preferred_element_type=F32) + b_ref[...])
    row = lax.broadcasted_iota(jnp.int32, (rows, rows), 0)
    col = lax.broadcasted_iota(jnp.int32, (rows, rows), 1)
    tri = jnp.logical_and((row % n_seq) == (col % n_seq), col <= row).astype(F32)
    cum = jnp.dot(tri, lf, preferred_element_type=F32, precision=lax.Precision.HIGHEST)
    lf_ref[...] = lf[:, :n_heads]
    cum_ref[...] = cum[:, :n_heads]


def _logf_sample(h, w, gate_col, b_f, n_seq, n_heads):
    rows, d = h.shape
    assert gate_col % LANES == 0
    return pl.pallas_call(
        functools.partial(_logf_sample_kernel, rows=rows, n_seq=n_seq, n_heads=n_heads),
        grid=(1,),
        in_specs=[pl.BlockSpec((rows, d), lambda i: (0, 0)),
                  pl.BlockSpec((LANES, d), lambda i: (gate_col // LANES, 0)),
                  pl.BlockSpec((1, LANES), lambda i: (0, 0))],
        out_specs=[pl.BlockSpec((rows, n_heads), lambda i: (0, 0))] * 2,
        out_shape=[jax.ShapeDtypeStruct((rows, n_heads), F32)] * 2,
        compiler_params=_params(1),
        name="logf_sample",
    )(h, w, b_f)


def _page_suffix_kernel(x_ref, after_ref, total_ref, *, n_heads):
    x = x_ref[...]
    n_sub = x.shape[1]
    lane = lax.broadcasted_iota(jnp.int32, x.shape, 2)
    sub = lax.broadcasted_iota(jnp.int32, x.shape, 1)
    incl = x
    rows = x
    sh = n_heads
    while sh < LANES:
        incl = incl + jnp.where(lane + sh < LANES, pltpu.roll(incl, LANES - sh, axis=2), 0.0)
        rows = rows + pltpu.roll(rows, sh, axis=2)
        sh *= 2
    below = rows
    total = rows
    sh = 1
    while sh < n_sub:
        below = below + jnp.where(sub + sh < n_sub, pltpu.roll(below, n_sub - sh, axis=1), 0.0)
        total = total + pltpu.roll(total, sh, axis=1)
        sh *= 2
    after_ref[...] = (incl - x) + (below - rows)
    total_ref[...] = total


def _page_suffix(logf_pages, n_heads):
    n_phys, n_sub, _ = logf_pages.shape
    tp = 256
    blk = pl.BlockSpec((tp, n_sub, LANES), lambda i: (i, 0, 0))
    return pl.pallas_call(
        functools.partial(_page_suffix_kernel, n_heads=n_heads),
        grid=(n_phys // tp,),
        in_specs=[blk],
        out_specs=[blk, blk],
        out_shape=[jax.ShapeDtypeStruct(logf_pages.shape, F32)] * 2,
        compiler_params=_params(1),
        name="page_suffix",
    )(logf_pages)


def _pool_prompt_kernel(u_ref, halo_ref, wp_ref, ps_ref, o_ref, ext_ref, *, tm, group):
    t = pl.program_id(1)
    ext_ref[0:HALO, :] = jnp.where(t > 0, halo_ref[...], 0.0)
    ext_ref[HALO:HALO + tm, :] = u_ref[...]
    pos = t * tm + lax.broadcasted_iota(jnp.int32, (tm, 1), 0)
    for g, w in enumerate(POOL_WINDOWS):
        cs = slice(g * group, (g + 1) * group)
        cur = u_ref[:, cs]
        win = cur
        for j in range(1, w):
            win = win + ext_ref[HALO - j:HALO - j + tm, cs]
        cnt = jnp.minimum(pos + 1, w).astype(F32)
        d = win / cnt - cur
        y = jnp.dot(d.astype(BF16), wp_ref[g], preferred_element_type=F32) * ps_ref[:, cs]
        o_ref[:, cs] = y.astype(BF16)


def _pool_prompt(u, w_pool_bf, pool_scale, seq):
    m, width = u.shape
    n_groups, group, _ = w_pool_bf.shape
    tm = 256
    tiles = seq // tm
    return pl.pallas_call(
        functools.partial(_pool_prompt_kernel, tm=tm, group=group),
        grid=(m // seq, tiles),
        in_specs=[pl.BlockSpec((tm, width), lambda b, t: (b * tiles + t, 0)),
                  pl.BlockSpec((HALO, width),
                               lambda b, t: (jnp.maximum((b * tiles + t) * (tm // HALO) - 1, 0), 0)),
                  pl.BlockSpec((n_groups, group, group), lambda b, t: (0, 0, 0)),
                  pl.BlockSpec((1, width), lambda b, t: (0, 0))],
        out_specs=pl.BlockSpec((tm, width), lambda b, t: (b * tiles + t, 0)),
        out_shape=jax.ShapeDtypeStruct((m, width), BF16),
        scratch_shapes=[pltpu.VMEM((HALO + tm, width), F32)],
        compiler_params=_params(2),
        name="pool_prompt",
    )(u, u, w_pool_bf, pool_scale.reshape(1, width))


def _pool_sample_kernel(ext_ref, wp_ref, ps_ref, o_ref, *, t_new, n_seq):
    g = pl.program_id(0)
    for gi, w in enumerate(POOL_WINDOWS):
        @pl.when(g == gi)
        def _(w=w):
            ds = []
            for t in range(t_new):
                cur = ext_ref[POOL_HIST + t]
                win = cur
                for j in range(1, w):
                    win = win + ext_ref[POOL_HIST + t - j]
                ds.append(win / float(w) - cur)
            d = jnp.concatenate(ds, axis=0).astype(BF16)
            y = jnp.dot(d, wp_ref[...], preferred_element_type=F32) * ps_ref[...]
            for t in range(t_new):
                o_ref[t] = y[t * n_seq:(t + 1) * n_seq].astype(BF16)


def _pool_sample(ext_t, w_pool_bf, pool_scale, t_new):
    rows, n_seq, width = ext_t.shape
    n_groups, group, _ = w_pool_bf.shape
    return pl.pallas_call(
        functools.partial(_pool_sample_kernel, t_new=t_new, n_seq=n_seq),
        grid=(n_groups,),
        in_specs=[pl.BlockSpec((rows, n_seq, group), lambda g: (0, 0, g)),
                  pl.BlockSpec((None, group, group), lambda g: (g, 0, 0)),
                  pl.BlockSpec((1, group), lambda g: (0, g))],
        out_specs=pl.BlockSpec((t_new, n_seq, group), lambda g: (0, 0, g)),
        out_shape=jax.ShapeDtypeStruct((t_new, n_seq, width), BF16),
        compiler_params=_params(1),
        name="pool_sample",
    )(ext_t, w_pool_bf, pool_scale.reshape(1, width))


def _softmax_step(s, v, m_ref, l_ref, acc_ref):
    m_prev = m_ref[...]
    m_new = jnp.maximum(m_prev, jnp.max(s, axis=-1, keepdims=True))
    alpha = jnp.exp(m_prev - m_new)
    p = jnp.exp(s - m_new)
    l_ref[...] = alpha * l_ref[...] + jnp.sum(p, axis=-1, keepdims=True)
    acc_ref[...] = alpha * acc_ref[...] + jnp.dot(p.astype(BF16), v, preferred_element_type=F32)
    m_ref[...] = m_new


def _attn_prompt_kernel(q_ref, kf_ref, vf_ref, frow_ref, fcol_ref, o_ref, k_ref, v_ref,
                        *, seq, tq, n_heads):
    h = pl.program_id(1)
    k_ref[...] = kf_ref[...].astype(BF16)
    v_ref[...] = vf_ref[...].astype(BF16)
    lane = lax.broadcasted_iota(jnp.int32, (seq, n_heads), 1)
    fcol_all = jnp.sum(jnp.where(lane == h, fcol_ref[...], 0.0), axis=1, keepdims=True) * LOG2E
    frow_all = frow_ref[pl.ds(h, 1), :] * LOG2E
    row = lax.broadcasted_iota(jnp.int32, (tq, tq), 0)
    col = lax.broadcasted_iota(jnp.int32, (tq, tq), 1)
    causal = col <= row

    def scores(q, lo, hi):
        s = lax.dot_general(q, k_ref[lo:hi, :], NT_DIMS, preferred_element_type=F32)
        return s * (ATTN_SCALE * LOG2E) - frow_all[:, lo:hi]

    for qi in range(seq // tq):
        lo, hi = qi * tq, (qi + 1) * tq
        q = q_ref[lo:hi, :]
        fcol = fcol_all[lo:hi]
        t_diag = jnp.where(causal, scores(q, lo, hi), NEG_INF)
        t_max = jnp.max(t_diag, axis=-1, keepdims=True)
        if qi > 0:
            t_past = scores(q, 0, lo)
            t_max = jnp.maximum(t_max, jnp.max(t_past, axis=-1, keepdims=True))
        m = t_max + fcol
        shift = fcol - m
        if qi > 0:
            p = jnp.exp2(t_past + shift)
            l = jnp.sum(p, axis=-1, keepdims=True)
            acc = jnp.dot(p.astype(BF16), v_ref[0:lo, :], preferred_element_type=F32)
        p = jnp.exp2(t_diag + shift)
        if qi > 0:
            l = l + jnp.sum(p, axis=-1, keepdims=True)
            acc = acc + jnp.dot(p.astype(BF16), v_ref[lo:hi, :], preferred_element_type=F32)
        else:
            l = jnp.sum(p, axis=-1, keepdims=True)
            acc = jnp.dot(p.astype(BF16), v_ref[lo:hi, :], preferred_element_type=F32)
        o_ref[lo:hi, :] = (acc / l).astype(BF16)


def _attn_prompt(q, k, v, cum_t, cum, seq, n_heads):
    m, width = q.shape
    tq = 256
    blk = pl.BlockSpec((seq, HEAD_DIM), lambda b, h: (b, h))
    return pl.pallas_call(
        functools.partial(_attn_prompt_kernel, seq=seq, tq=tq, n_heads=n_heads),
        grid=(m // seq, n_heads),
        in_specs=[blk, blk, blk,
                  pl.BlockSpec((None, n_heads, seq), lambda b, h: (b, 0, 0)),
                  pl.BlockSpec((seq, n_heads), lambda b, h: (b, 0))],
        out_specs=blk,
        out_shape=jax.ShapeDtypeStruct((m, width), BF16),
        scratch_shapes=[pltpu.VMEM((seq, HEAD_DIM), BF16), pltpu.VMEM((seq, HEAD_DIM), BF16)],
        compiler_params=_params(2),
        name="attn_prompt",
    )(q, k, v, cum_t, cum)


def _attn_sample_kernel(pt_ref, q_ref, k_hbm, v_hbm, *refs, n_pp, t_new, n_heads, n_pages):
    after_refs = refs[:n_pp]
    total_refs = refs[n_pp:2 * n_pp]
    kn_ref, vn_ref, cnrow_ref, cncol_ref, o_ref = refs[2 * n_pp:2 * n_pp + 5]
    (mb_ref, m_ref, l_ref, acc_ref, knp_ref, vnp_ref, later_ref,
     kbuf_ref, vbuf_ref, sem_ref) = refs[2 * n_pp + 5:]
    c = pl.program_id(1)
    steps = pl.num_programs(1)
    n_rows = n_heads * t_new
    n_cols = PAGE_SIZE * n_heads

    step = pl.program_id(0) * steps + c
    n_steps = pl.num_programs(0) * steps

    def page_copies(g):
        slot = g % PAGE_RING
        seq, chunk = g // steps, g % steps
        copies = []
        for i in range(n_pp):
            page = pt_ref[seq, n_pages - 1 - (chunk * n_pp + i)]
            copies.append(pltpu.make_async_copy(k_hbm.at[page], kbuf_ref.at[slot, i],
                                                sem_ref.at[slot]))
            copies.append(pltpu.make_async_copy(v_hbm.at[page], vbuf_ref.at[slot, i],
                                                sem_ref.at[slot]))
        return copies

    @pl.when(step == 0)
    def _():
        for g in range(PAGE_RING - 1):
            for cp in page_copies(g):
                cp.start()

    @pl.when(step + PAGE_RING - 1 < n_steps)
    def _():
        for cp in page_copies(step + PAGE_RING - 1):
            cp.start()

    for cp in page_copies(step):
        cp.wait()
    slot = step % PAGE_RING
    k_refs = [kbuf_ref.at[slot, i] for i in range(n_pp)]
    v_refs = [vbuf_ref.at[slot, i] for i in range(n_pp)]

    @pl.when(c == 0)
    def _():
        row = lax.broadcasted_iota(jnp.int32, (n_rows, n_cols), 0)
        col = lax.broadcasted_iota(jnp.int32, (n_rows, n_cols), 1)
        same_head = (row // t_new) == (col % n_heads)
        mb_ref[...] = jnp.where(same_head, cncol_ref[...], NEG_INF)
        m_ref[...] = jnp.full_like(m_ref, NEG_INF)
        l_ref[...] = jnp.zeros_like(l_ref)
        acc_ref[...] = jnp.zeros_like(acc_ref)
        later_ref[...] = jnp.zeros_like(later_ref)

    q = q_ref[...]
    s_parts = []
    for i in range(n_pp):
        kp = k_refs[i][...].astype(BF16)
        s = lax.dot_general(q, kp, NT_DIMS, preferred_element_type=F32)
        bias = after_refs[i][...] + later_ref[...]
        later_ref[...] += total_refs[i][...]
        chunks = []
        for j in range(n_cols // LANES):
            cs = slice(j * LANES, (j + 1) * LANES)
            chunks.append(s[:, cs] * ATTN_SCALE + mb_ref[:, cs] + bias[j:j + 1, :])
        s_parts.append(jnp.concatenate(chunks, axis=1))
    m_prev = m_ref[...]
    m_new = m_prev
    for s in s_parts:
        m_new = jnp.maximum(m_new, jnp.max(s, axis=-1, keepdims=True))
    alpha = jnp.exp(m_prev - m_new)
    l_new = alpha * l_ref[...]
    acc = alpha * acc_ref[...]
    for i, s in enumerate(s_parts):
        p = jnp.exp(s - m_new)
        l_new = l_new + jnp.sum(p, axis=-1, keepdims=True)
        acc = acc + jnp.dot(p.astype(BF16), v_refs[i][...].astype(BF16),
                            preferred_element_type=F32)
    m_ref[...] = m_new
    l_ref[...] = l_new
    acc_ref[...] = acc

    @pl.when(c == pl.num_programs(1) - 1)
    def _():
        n_new = t_new * n_heads
        knp_ref[0:n_new, :] = kn_ref[...].astype(BF16)
        knp_ref[n_new:, :] = jnp.zeros((LANES - n_new, HEAD_DIM), BF16)
        vnp_ref[0:n_new, :] = vn_ref[...].astype(BF16)
        vnp_ref[n_new:, :] = jnp.zeros((LANES - n_new, HEAD_DIM), BF16)
        s = lax.dot_general(q, knp_ref[...], NT_DIMS, preferred_element_type=F32)
        row = lax.broadcasted_iota(jnp.int32, (n_rows, LANES), 0)
        col = lax.broadcasted_iota(jnp.int32, (n_rows, LANES), 1)
        valid = jnp.logical_and((row // t_new) == (col % n_heads),
                                (col // n_heads) <= (row % t_new))
        s = jnp.where(valid, s * ATTN_SCALE + cncol_ref[...] - cnrow_ref[...], NEG_INF)
        _softmax_step(s, vnp_ref[...], m_ref, l_ref, acc_ref)
        o_ref[...] = (acc_ref[...] / l_ref[...]).astype(BF16)


def _attn_sample(page_table, q_rows, k_pages, v_pages, lf_after, lf_total, k_new, v_new,
                 cn_row, cn_col, t_new, n_heads):
    n_seq, n_pages = page_table.shape
    n_rows = n_heads * t_new
    n_cols = PAGE_SIZE * n_heads
    n_sub = n_cols // LANES
    n_new = t_new * n_heads
    n_pp = 4
    steps = n_pages // n_pp

    def page_idx(b, c, pt, i):
        return pt[b, n_pages - 1 - (c * n_pp + i)]

    def lf_spec(i):
        return pl.BlockSpec((None, n_sub, LANES), lambda b, c, pt: (page_idx(b, c, pt, i), 0, 0))

    def seq_spec(r, w):
        return pl.BlockSpec((None, r, w), lambda b, c, pt: (b, 0, 0))

    return pl.pallas_call(
        functools.partial(_attn_sample_kernel, n_pp=n_pp, t_new=t_new, n_heads=n_heads,
                          n_pages=n_pages),
        grid_spec=pltpu.PrefetchScalarGridSpec(
            num_scalar_prefetch=1,
            grid=(n_seq, steps),
            in_specs=([seq_spec(n_rows, HEAD_DIM),
                       pl.BlockSpec(memory_space=pl.ANY), pl.BlockSpec(memory_space=pl.ANY)]
                      + [lf_spec(i) for i in range(n_pp)] * 2
                      + [seq_spec(n_new, HEAD_DIM), seq_spec(n_new, HEAD_DIM),
                         seq_spec(1, LANES), seq_spec(n_rows, 1)]),
            out_specs=seq_spec(n_rows, HEAD_DIM),
            scratch_shapes=[pltpu.VMEM((n_rows, n_cols), F32),
                            pltpu.VMEM((n_rows, 1), F32), pltpu.VMEM((n_rows, 1), F32),
                            pltpu.VMEM((n_rows, HEAD_DIM), F32),
                            pltpu.VMEM((LANES, HEAD_DIM), BF16),
                            pltpu.VMEM((LANES, HEAD_DIM), BF16),
                            pltpu.VMEM((n_sub, LANES), F32),
                            pltpu.VMEM((PAGE_RING, n_pp, n_cols, HEAD_DIM), F32),
                            pltpu.VMEM((PAGE_RING, n_pp, n_cols, HEAD_DIM), F32),
                            pltpu.SemaphoreType.DMA((PAGE_RING,))],
        ),
        out_shape=jax.ShapeDtypeStruct((n_seq, n_rows, HEAD_DIM), BF16),
        compiler_params=_params(2),
        name="attn_sample",
    )(page_table, q_rows, k_pages, v_pages,
      *([lf_after] * n_pp), *([lf_total] * n_pp), k_new, v_new, cn_row, cn_col)


def _outproj_kernel(a_ref, p_ref, wa_ref, wp_ref, x_ref, ga_ref, o_ref, wabf_ref, wpbf_ref):
    @pl.when(pl.program_id(1) == 0)
    def _():
        wabf_ref[...] = wa_ref[...].astype(BF16)
        wpbf_ref[...] = wp_ref[...].astype(BF16)

    y = jnp.dot(a_ref[...], wabf_ref[...], preferred_element_type=F32)
    y = y + jnp.dot(p_ref[...], wpbf_ref[...], preferred_element_type=F32)
    for rows in _row_groups(x_ref.shape[0], ga_ref):
        o_ref[rows, :] = x_ref[rows, :] + ga_ref[...] * y[rows, :]


def _outproj(a, pm, w_out, x, mod, rows_per_batch, n_seq, ga_chunk):
    m, ka = a.shape
    kp = pm.shape[1]
    d = w_out.shape[1]
    tm, tn = min(m, 1024), 512
    tiles = rows_per_batch // tm
    ga_col0 = ga_chunk * (d // tn)
    return pl.pallas_call(
        _outproj_kernel,
        grid=(d // tn, m // tm),
        in_specs=[pl.BlockSpec((tm, ka), lambda j, i: (i, 0)),
                  pl.BlockSpec((tm, kp), lambda j, i: (i, 0)),
                  pl.BlockSpec((ka, tn), lambda j, i: (0, j)),
                  pl.BlockSpec((kp, tn), lambda j, i: (ka // kp, j)),
                  pl.BlockSpec((tm, tn), lambda j, i: (i, j)),
                  pl.BlockSpec(_mod_block(mod, tn, n_seq),
                               lambda j, i: _mod_index(mod, i // tiles, ga_col0 + j))],
        out_specs=pl.BlockSpec((tm, tn), lambda j, i: (i, j)),
        out_shape=jax.ShapeDtypeStruct((m, d), F32),
        scratch_shapes=[pltpu.VMEM((ka, tn), BF16), pltpu.VMEM((kp, tn), BF16)],
        compiler_params=_params(2),
        name="out_proj",
    )(a, pm, w_out, w_out, x, mod)


def _ffn_kernel(h_ref, wu_ref, wd_ref, x_ref, ga_ref, o_ref, hid_ref, *, n_up, tf):
    s = pl.program_id(1)

    @pl.when(s < n_up)
    def _():
        a = jnp.dot(h_ref[...], wu_ref[...], preferred_element_type=F32)
        col = pl.multiple_of(s * tf, tf)
        hid_ref[:, pl.ds(col, tf)] = jnp.square(jnp.maximum(a, 0.0)).astype(BF16)

    @pl.when(s >= n_up)
    def _():
        y = jnp.dot(hid_ref[...], wd_ref[...], preferred_element_type=F32)
        for rows in _row_groups(x_ref.shape[0], ga_ref):
            o_ref[rows, :] = x_ref[rows, :] + ga_ref[...] * y[rows, :]


def _ffn(h, x, w_up_bf, w_down_bf, mod, rows_per_batch, n_seq, ga_chunk):
    m, d = x.shape
    d_ff = w_up_bf.shape[1]
    tm, tf, tn = 512, 512, 256
    tiles = rows_per_batch // tm
    n_up, n_dn = d_ff // tf, d // tn
    ga_col0 = ga_chunk * n_dn

    def up_step(s):
        return jnp.minimum(s, n_up - 1)

    def dn_step(s):
        return jnp.maximum(s - n_up, 0)

    return pl.pallas_call(
        functools.partial(_ffn_kernel, n_up=n_up, tf=tf),
        grid=(m // tm, n_up + n_dn),
        in_specs=[pl.BlockSpec((tm, d), lambda i, s: (i, 0)),
                  pl.BlockSpec((d, tf), lambda i, s: (0, up_step(s))),
                  pl.BlockSpec((d_ff, tn), lambda i, s: (0, dn_step(s))),
                  pl.BlockSpec((tm, tn), lambda i, s: (i, dn_step(s))),
                  pl.BlockSpec(_mod_block(mod, tn, n_seq),
                               lambda i, s: _mod_index(mod, i // tiles, ga_col0 + dn_step(s)))],
        out_specs=pl.BlockSpec((tm, tn), lambda i, s: (i, dn_step(s))),
        out_shape=jax.ShapeDtypeStruct((m, d), F32),
        scratch_shapes=[pltpu.VMEM((tm, d_ff), BF16)],
        compiler_params=_params(2),
        name="ffn",
    )(h, w_up_bf, w_down_bf, x, mod)


def _pad_cols(w, n):
    return jnp.pad(w, ((0, 0), (0, n - w.shape[1])))


def kernel(x_prompt, x_sample, c_prompt, c_sample, cache_k, cache_v, cache_logf, state_pool,
           page_table, w_ada, b_ada, g_mix, w_in, b_f, w_pool, pool_scale, w_out, g_ffn,
           w_up, w_down, g_final):
    n_b, seq, d = x_prompt.shape
    n_s, t_new, _ = x_sample.shape
    assert w_ada.shape[0] == 1, "single trunk layer"
    l = 0
    n_heads = b_f.shape[1]
    attn_w = n_heads * HEAD_DIM
    pool_w = pool_scale.shape[1]
    n_phys = cache_k.shape[1]
    n_pages = page_table.shape[1]
    n_cols = PAGE_SIZE * n_heads
    SH1, SC1, GA1, SH2, SC2, GA2 = range(N_ADA)

    w_in_l = w_in[l].T
    gate_col = 3 * attn_w
    b_f_p = _pad_cols(b_f[l].reshape(1, n_heads), LANES)
    w_pool_bf = w_pool[l].astype(BF16)
    w_up_bf = w_up[l].astype(BF16)
    w_down_bf = w_down[l].astype(BF16)

    n_c = n_s + n_b
    c_rows = -(-n_c // 16) * 16
    c_all = jnp.pad(jnp.concatenate([c_sample, c_prompt], axis=0), ((0, c_rows - n_c), (0, 0)))
    mod_s = _modulation(c_all, w_ada[l], b_ada[l])
    mod_p = mod_s[n_s:n_c].reshape(n_b, 1, N_ADA * d)

    def trunk_front(x2, mod_x, rows_per_batch):
        h = _norm_mod(x2, g_mix[l], mod_x, rows_per_batch, n_s, SC1, SH1)
        (q,) = _proj(h, w_in_l, 0, attn_w, [BF16], "proj_q")
        (k,) = _proj(h, w_in_l, attn_w, attn_w, [F32], "proj_k")
        (v,) = _proj(h, w_in_l, 2 * attn_w, attn_w, [F32], "proj_v")
        (u,) = _proj(h, w_in_l, gate_col + n_heads, pool_w, [F32], "proj_u")
        return h, q, k, v, u

    def trunk_back(a, pm, x2, mod_x, rows_per_batch):
        x1 = _outproj(a, pm, w_out[l], x2, mod_x, rows_per_batch, n_s, GA1)
        h2 = _norm_mod(x1, g_ffn[l], mod_x, rows_per_batch, n_s, SC2, SH2)
        x3 = _ffn(h2, x1, w_up_bf, w_down_bf, mod_x, rows_per_batch, n_s, GA2)
        return _final_norm(x3, g_final)

    xp = x_prompt.reshape(n_b * seq, d)
    h, q, k_p, v_p, u_p = trunk_front(xp, mod_p, seq)
    logf_p, cum_p = _logf_prompt(h, w_in_l, gate_col, b_f_p, seq, n_heads)
    cum_t = cum_p.reshape(n_b, seq, n_heads).transpose(0, 2, 1)
    a_p = _attn_prompt(q, k_p, v_p, cum_t, cum_p, seq, n_heads)
    pm_p = _pool_prompt(u_p, w_pool_bf, pool_scale[l], seq)
    y_p = trunk_back(a_p, pm_p, xp, mod_p, seq)
    hist_p = u_p.reshape(n_b, seq, pool_w)[:, seq - POOL_HIST:]

    def seq_major(x2):
        return x2.reshape(t_new, n_s, x2.shape[-1]).transpose(1, 0, 2)

    xs = x_sample.transpose(1, 0, 2).reshape(t_new * n_s, d)
    h, q, k_s, v_s, u_s = trunk_front(xs, mod_s, t_new * n_s)
    logf_s, cn = _logf_sample(h, w_in_l, gate_col, b_f_p, n_s, n_heads)
    k_s, v_s, logf_s, cn = seq_major(k_s), seq_major(v_s), seq_major(logf_s), seq_major(cn)

    lf_after, lf_total = _page_suffix(
        cache_logf[l].reshape(n_phys, n_cols // LANES, LANES), n_heads)
    cn_row = _pad_cols(cn.reshape(n_s, t_new * n_heads), LANES).reshape(n_s, 1, LANES)
    cn_col = cn.transpose(0, 2, 1).reshape(n_s, n_heads * t_new, 1)
    q_rows = q.reshape(t_new, n_s, n_heads, HEAD_DIM).transpose(1, 2, 0, 3)
    q_rows = q_rows.reshape(n_s, n_heads * t_new, HEAD_DIM)
    a_s = _attn_sample(page_table, q_rows,
                       cache_k[l].reshape(n_phys, n_cols, HEAD_DIM),
                       cache_v[l].reshape(n_phys, n_cols, HEAD_DIM),
                       lf_after, lf_total,
                       k_s.reshape(n_s, t_new * n_heads, HEAD_DIM),
                       v_s.reshape(n_s, t_new * n_heads, HEAD_DIM),
                       cn_row, cn_col, t_new, n_heads)
    a_s = a_s.reshape(n_s, n_heads, t_new, HEAD_DIM).transpose(2, 0, 1, 3)
    a_s = a_s.reshape(t_new * n_s, attn_w)

    ext_t = jnp.concatenate([state_pool[l].transpose(1, 0, 2),
                             u_s.reshape(t_new, n_s, pool_w)], axis=0)
    pm_s = _pool_sample(ext_t, w_pool_bf, pool_scale[l], t_new).reshape(t_new * n_s, pool_w)
    y_s = trunk_back(a_s, pm_s, xs, mod_s, t_new * n_s)
    hist_s = ext_t[t_new:].transpose(1, 0, 2)

    kv_p = (1, n_b, seq, n_heads, HEAD_DIM)
    kv_s = (1, n_s, t_new, n_heads, HEAD_DIM)
    return (y_p.reshape(n_b, seq, d), seq_major(y_s),
            k_p.reshape(kv_p), v_p.reshape(kv_p), logf_p.reshape(1, n_b, seq, n_heads), hist_p[None],
            k_s.reshape(kv_s), v_s.reshape(kv_s), logf_s[None], hist_s[None])
```
